```python
import math
import jax
import jax.numpy as jnp
from jax import lax
import numpy as np

D_MODEL = 2048
BATCH = 16
SEQ = 2048
DEPTH = 4

GRID_W = 64
CTX_LEN = 256
N_DIFF_HEADS = 8
DIFF_HD = 64
DIFF_VD = 2 * DIFF_HD
ATTN_W = N_DIFF_HEADS * DIFF_VD
QK_W = N_DIFF_HEADS * 2 * DIFF_HD
Q_BLOCK = 128
ROPE_THETA = 10000.0
ROPE_FREQS = DIFF_HD // 4
CONV_W = 512
CONV_K = 31
FOURIER_HEADS = 4
FOURIER_HD = 128
FOURIER_W = FOURIER_HEADS * FOURIER_HD
MIX_W = ATTN_W + CONV_W + FOURIER_W
K0 = QK_W
V0 = 2 * QK_W
G0 = V0 + ATTN_W
F0 = G0 + 2 * CONV_W
IN_W = F0 + FOURIER_W
N_EXPERTS = 16
EXPERT_FF = 1024
CAPACITY_FACTOR = 2
EPS = 1e-6

kernel_name = 'hybrid_diffattn_conformer_fnet_ecmoe_dit'


def rms_norm(x, g):
    x32 = x.astype(jnp.float32)
    y = x32 * lax.rsqrt(jnp.mean(x32 * x32, axis=-1, keepdims=True) + EPS)
    return (y * g.astype(jnp.float32)).astype(x.dtype)


def layer_norm(x, g, b):
    x32 = x.astype(jnp.float32)
    mu = jnp.mean(x32, axis=-1, keepdims=True)
    var = jnp.mean(jnp.square(x32 - mu), axis=-1, keepdims=True)
    y = (x32 - mu) * lax.rsqrt(var + EPS)
    return (y * g.astype(jnp.float32) + b.astype(jnp.float32)).astype(x.dtype)


def modulate(h, shift, scale):
    return h * (1 + scale) + shift


def rope_tables(n, dtype):
    rows = n // GRID_W
    row = jnp.repeat(jnp.arange(rows), GRID_W)
    col = jnp.tile(jnp.arange(GRID_W), rows)
    pos = jnp.stack([row, col], axis=-1).astype(jnp.float32)
    inv_freq = ROPE_THETA ** (-jnp.arange(ROPE_FREQS, dtype=jnp.float32) / ROPE_FREQS)
    ang = pos[:, :, None, None] * inv_freq
    ang = jnp.broadcast_to(ang, (n, 2, 2, ROPE_FREQS)).reshape(n, DIFF_HD)
    return jnp.cos(ang).astype(dtype), jnp.sin(ang).astype(dtype)


def apply_rope_2d(t, cos, sin):
    ts = t.reshape(t.shape[:-1] + (2, 2, ROPE_FREQS))
    rot = jnp.stack([-ts[..., 1, :], ts[..., 0, :]], axis=-2).reshape(t.shape)
    return t * cos[None, :, None, None, :] + rot * sin[None, :, None, None, :]


def split_mix(u):
    b, n = u.shape[:2]
    q = u[..., :K0].reshape(b, n, N_DIFF_HEADS, 2, DIFF_HD)
    k = u[..., K0:V0].reshape(b, n, N_DIFF_HEADS, 2, DIFF_HD)
    v = u[..., V0:G0].reshape(b, n, N_DIFF_HEADS, DIFF_VD)
    return q, k, v, u[..., G0:F0], u[..., F0:]


def diff_lambda_value(lv, lam_init):
    lv = lv.astype(jnp.float32)
    return jnp.exp(jnp.sum(lv[0] * lv[1])) - jnp.exp(jnp.sum(lv[2] * lv[3])) + lam_init


def diff_attn_core(q, k, v, lam):
    s = jnp.einsum('bqhmd,bkhmd->bhmqk', q, k).astype(jnp.float32) * (DIFF_HD ** -0.5)
    p = jax.nn.softmax(s, axis=-1)
    a = p[:, :, 0] - lam * p[:, :, 1]
    return jnp.einsum('bhqk,bkhe->bqhe', a.astype(v.dtype), v)


def diff_head_out(o, g_sub, lam_init):
    b, n = o.shape[:2]
    return (rms_norm(o, g_sub) * (1.0 - lam_init)).reshape(b, n, ATTN_W)


def diff_attention_latent(q, k, v, kc, vc, lam):
    b, n = q.shape[:2]
    k_all = jnp.concatenate([kc, k], axis=1)
    v_all = jnp.concatenate([vc, v], axis=1)
    nb = n // Q_BLOCK
    qb = jnp.swapaxes(q.reshape((b, nb, Q_BLOCK) + q.shape[2:]), 0, 1)
    o = lax.map(lambda blk: diff_attn_core(blk, k_all, v_all, lam), qb)
    return jnp.swapaxes(o, 0, 1).reshape(b, n, N_DIFF_HEADS, DIFF_VD)


def conformer_conv(u, w, bias, ln_g, ln_b):
    a, g = jnp.split(u, 2, axis=-1)
    z = a * jax.nn.sigmoid(g)
    z = lax.conv_general_dilated(z, w[:, None, :], window_strides=(1,),
                                 padding=[(CONV_K // 2, CONV_K // 2)],
                                 dimension_numbers=('NWC', 'WIO', 'NWC'),
                                 feature_group_count=CONV_W) + bias
    return jax.nn.silu(layer_norm(z, ln_g, ln_b))


def fourier_mix(u):
    b, n = u.shape[:2]
    z = u.astype(jnp.float32).reshape(b, n, FOURIER_HEADS, FOURIER_HD)
    z = jnp.fft.fftn(z, axes=(1, 3), norm='ortho').real
    return z.reshape(b, n, FOURIER_W).astype(u.dtype)


def expert_choice_ffn(h, w_r, w_g, w_u, w_d):
    b, n, d = h.shape
    cap = (CAPACITY_FACTOR * n) // N_EXPERTS
    aff = jax.nn.softmax((h @ w_r).astype(jnp.float32), axis=-1)
    gates, idx = lax.top_k(jnp.swapaxes(aff, 1, 2), cap)
    xs = jax.vmap(lambda hb, ib: hb[ib])(h, idx)
    hid = jax.nn.silu(jnp.einsum('becd,edf->becf', xs, w_g)) * jnp.einsum('becd,edf->becf', xs, w_u)
    y = jnp.einsum('becf,efd->becd', hid, w_d) * gates[..., None].astype(h.dtype)
    scatter = lambda ib, yb: jnp.zeros((n, d), yb.dtype).at[ib.reshape(-1)].add(yb.reshape(-1, d))
    return jax.vmap(scatter)(idx, y)


def setup_inputs(seed: int = 0) -> dict:
    key = jax.random.key(seed)
    ks = jax.random.split(key, 24)
    nrm = lambda k, shape, s: jax.random.normal(k, shape, jnp.float32) * s
    return {
        'x': nrm(ks[0], (BATCH, SEQ, D_MODEL), 1.0),
        'c': nrm(ks[1], (BATCH, D_MODEL), 1.0),
        'ctx': nrm(ks[2], (BATCH, CTX_LEN, D_MODEL), 1.0),
        'c_ctx': nrm(ks[3], (D_MODEL,), 1.0),
        'w_ada': nrm(ks[4], (DEPTH, D_MODEL, 6 * D_MODEL), 0.5 * D_MODEL ** -0.5),
        'b_ada': nrm(ks[5], (DEPTH, 6 * D_MODEL), 0.02),
        'g_norm1': 1.0 + nrm(ks[6], (DEPTH, D_MODEL), 0.02),
        'w_in': nrm(ks[7], (DEPTH, D_MODEL, IN_W), D_MODEL ** -0.5),
        'diff_lambda': nrm(ks[8], (DEPTH, 4, DIFF_HD), 0.1),
        'g_sub': 1.0 + nrm(ks[9], (DEPTH, DIFF_VD), 0.02),
        'conv_w': nrm(ks[10], (DEPTH, CONV_K, CONV_W), CONV_K ** -0.5),
        'conv_b': nrm(ks[11], (DEPTH, CONV_W), 0.02),
        'conv_ln_g': 1.0 + nrm(ks[12], (DEPTH, CONV_W), 0.02),
        'conv_ln_b': nrm(ks[13], (DEPTH, CONV_W), 0.02),
        'w_out': nrm(ks[14], (DEPTH, MIX_W, D_MODEL), MIX_W ** -0.5),
        'g_norm2': 1.0 + nrm(ks[15], (DEPTH, D_MODEL), 0.02),
        'w_router': nrm(ks[16], (DEPTH, D_MODEL, N_EXPERTS), D_MODEL ** -0.5),
        'w_gate': nrm(ks[17], (DEPTH, N_EXPERTS, D_MODEL, EXPERT_FF), D_MODEL ** -0.5),
        'w_up': nrm(ks[18], (DEPTH, N_EXPERTS, D_MODEL, EXPERT_FF), D_MODEL ** -0.5),
        'w_down': nrm(ks[19], (DEPTH, N_EXPERTS, EXPERT_FF, D_MODEL), EXPERT_FF ** -0.5),
        'g_final': 1.0 + nrm(ks[20], (D_MODEL,), 0.02),
    }


def reference(x, c, ctx, c_ctx, w_ada, b_ada, g_norm1, w_in, diff_lambda, g_sub, conv_w, conv_b,
              conv_ln_g, conv_ln_b, w_out, g_norm2, w_router, w_gate, w_up, w_down, g_final):
    n = x.shape[1]
    cos, sin = rope_tables(n, x.dtype)
    c_act = jax.nn.silu(c)
    cc_act = jax.nn.silu(c_ctx)
    xc = ctx
    for l in range(DEPTH):
        last = l == DEPTH - 1
        mod = (c_act @ w_ada[l] + b_ada[l])[:, None, :]
        modc = cc_act @ w_ada[l] + b_ada[l]
        sh1, sc1, gt1, sh2, sc2, gt2 = jnp.split(mod, 6, axis=-1)
        csh1, csc1, cgt1, csh2, csc2, cgt2 = jnp.split(modc, 6, axis=-1)
        lam_init = 0.8 - 0.6 * math.exp(-0.3 * l)
        lam = diff_lambda_value(diff_lambda[l], lam_init)

        h = modulate(rms_norm(x, g_norm1[l]), sh1, sc1)
        hc = modulate(rms_norm(xc, g_norm1[l]), csh1, csc1)
        q, k, v, u_glu, u_four = split_mix(h @ w_in[l])
        if last:
            ukv = hc @ w_in[l][:, K0:G0]
            kc = ukv[..., :QK_W].reshape(ukv.shape[:2] + (N_DIFF_HEADS, 2, DIFF_HD))
            vc = ukv[..., QK_W:].reshape(ukv.shape[:2] + (N_DIFF_HEADS, DIFF_VD))
        else:
            qc, kc, vc, uc_glu, uc_four = split_mix(hc @ w_in[l])
        q = apply_rope_2d(q, cos, sin)
        k = apply_rope_2d(k, cos, sin)
        attn = diff_head_out(diff_attention_latent(q, k, v, kc, vc, lam), g_sub[l], lam_init)
        conv = conformer_conv(u_glu, conv_w[l], conv_b[l], conv_ln_g[l], conv_ln_b[l])
        four = fourier_mix(u_four)
        y = jnp.concatenate([attn, conv, four], axis=-1) @ w_out[l]
        if not last:
            attn_c = diff_head_out(diff_attn_core(qc, kc, vc, lam), g_sub[l], lam_init)
            conv_c = conformer_conv(uc_glu, conv_w[l], conv_b[l], conv_ln_g[l], conv_ln_b[l])
            four_c = fourier_mix(uc_four)
            yc = jnp.concatenate([attn_c, conv_c, four_c], axis=-1) @ w_out[l]
            xc = xc + cgt1 * yc
        x = x + gt1 * y

        h2 = modulate(rms_norm(x, g_norm2[l]), sh2, sc2)
        x = x + gt2 * expert_choice_ffn(h2, w_router[l], w_gate[l], w_up[l], w_down[l])
        if not last:
            h2c = modulate(rms_norm(xc, g_norm2[l]), csh2, csc2)
            xc = xc + cgt2 * expert_choice_ffn(h2c, w_router[l], w_gate[l], w_up[l], w_down[l])
    return rms_norm(x, g_final)
```

```python
import functools
import math

import jax
import jax.numpy as jnp
from jax import lax
from jax.experimental import pallas as pl
from jax.experimental.pallas import tpu as pltpu

D_MODEL = 2048
DEPTH = 4
GRID_W = 64
N_HEADS = 8
HEAD_D = 64
HEAD_W = 2 * HEAD_D
ATTN_W = N_HEADS * HEAD_W
QK_W = ATTN_W
ROPE_THETA = 10000.0
ROPE_FREQS = HEAD_D // 4
CONV_W = 512
CONV_K = 31
CONV_PAD = 16
FOUR_HEADS = 4
FOUR_HD = 128
FOUR_W = FOUR_HEADS * FOUR_HD
K0 = QK_W
V0 = 2 * QK_W
G0 = V0 + ATTN_W
F0 = G0 + 2 * CONV_W
IN_W = F0 + FOUR_W
N_EXPERTS = 16
EXPERT_FF = 1024
CAPACITY_FACTOR = 2
EPS = 1e-6
ROUTER_LANES = 128
HALF_D = D_MODEL // 2

VMEM_LIMIT = 56 * 1024 * 1024

F32 = jnp.float32
BF16 = jnp.bfloat16


def _params(sem, vmem=VMEM_LIMIT):
    return pltpu.CompilerParams(dimension_semantics=sem, vmem_limit_bytes=vmem)


def _resident(shape, index_map):
    return pl.BlockSpec(shape, index_map, pipeline_mode=pl.Buffered(1))


def _ada_kernel(c_ref, w_ref, b_ref, o_ref):
    c = c_ref[...]
    a = (c * jax.nn.sigmoid(c)).astype(BF16)
    w = w_ref[0].astype(BF16)
    o_ref[0] = jnp.dot(a, w, preferred_element_type=F32) + b_ref[0]


def ada_modulation(cc, w_ada, b_ada):
    rows = cc.shape[0]
    tn = 1024
    return pl.pallas_call(
        _ada_kernel,
        grid=(DEPTH, 6 * D_MODEL // tn),
        in_specs=[
            pl.BlockSpec((rows, D_MODEL), lambda l, j: (0, 0)),
            pl.BlockSpec((1, D_MODEL, tn), lambda l, j: (l, 0, j)),
            pl.BlockSpec((1, 1, tn), lambda l, j: (l, 0, j)),
        ],
        out_specs=pl.BlockSpec((1, rows, tn), lambda l, j: (l, 0, j)),
        out_shape=jax.ShapeDtypeStruct((DEPTH, rows, 6 * D_MODEL), F32),
        compiler_params=_params(("arbitrary", "arbitrary")),
        name="ada_modulation",
    )(cc, w_ada, b_ada.reshape(DEPTH, 1, 6 * D_MODEL))


def _rms_mod(x, g, sc, sh):
    ms = jnp.mean(x * x, axis=-1, keepdims=True)
    return (x * lax.rsqrt(ms + EPS)) * g * (1.0 + sc) + sh


def _inproj_kernel(x_ref, sh_ref, sc_ref, g_ref, w_ref, cos_ref, sa_ref, sb_ref,
                   q_ref, k_ref, v_ref, z_ref, f_ref):
    hb = _rms_mod(x_ref[0], g_ref[...], sc_ref[0], sh_ref[0]).astype(BF16)
    cos = cos_ref[...]
    sa = sa_ref[...]
    sb = sb_ref[...]

    def rope_store(col0, out_ref, scale):
        t = jnp.dot(hb, w_ref[:, col0:col0 + QK_W], preferred_element_type=F32)
        for h in range(N_HEADS):
            th = t[:, h * HEAD_W:(h + 1) * HEAD_W]
            r = th * cos + pltpu.roll(th, HEAD_W - ROPE_FREQS, 1) * sa + pltpu.roll(th, ROPE_FREQS, 1) * sb
            out_ref[0, :, h * HEAD_W:(h + 1) * HEAD_W] = (r * scale).astype(BF16)

    rope_store(0, q_ref, HEAD_D ** -0.5)
    rope_store(K0, k_ref, 1.0)
    v_ref[0] = jnp.dot(hb, w_ref[:, V0:G0], preferred_element_type=F32).astype(BF16)
    a = jnp.dot(hb, w_ref[:, G0:G0 + CONV_W], preferred_element_type=F32)
    g = jnp.dot(hb, w_ref[:, G0 + CONV_W:F0], preferred_element_type=F32)
    z_ref[0] = a * jax.nn.sigmoid(g)
    f_ref[0] = jnp.dot(hb, w_ref[:, F0:IN_W], preferred_element_type=F32).astype(BF16)


def inproj(x, sh, sc, g, w_in_bf, cos, sa, sb, tm=256):
    b, n, _ = x.shape
    row = lambda bi, i: (bi, i, 0)
    vec = lambda bi, i: (bi, 0, 0)
    tab = lambda bi, i: (i, 0)
    out = lambda w, dt: jax.ShapeDtypeStruct((b, n, w), dt)
    return pl.pallas_call(
        _inproj_kernel,
        grid=(b, n // tm),
        in_specs=[
            pl.BlockSpec((1, tm, D_MODEL), row),
            pl.BlockSpec((1, 1, D_MODEL), vec),
            pl.BlockSpec((1, 1, D_MODEL), vec),
            pl.BlockSpec((1, D_MODEL), lambda bi, i: (0, 0)),
            _resident((D_MODEL, IN_W), lambda bi, i: (0, 0)),
            pl.BlockSpec((tm, HEAD_W), tab),
            pl.BlockSpec((tm, HEAD_W), tab),
            pl.BlockSpec((tm, HEAD_W), tab),
        ],
        out_specs=[
            pl.BlockSpec((1, tm, QK_W), row),
            pl.BlockSpec((1, tm, QK_W), row),
            pl.BlockSpec((1, tm, ATTN_W), row),
            pl.BlockSpec((1, tm, CONV_W), row),
            pl.BlockSpec((1, tm, FOUR_W), row),
        ],
        out_shape=[out(QK_W, BF16), out(QK_W, BF16), out(ATTN_W, BF16), out(CONV_W, F32), out(FOUR_W, BF16)],
        compiler_params=_params(("arbitrary", "arbitrary")),
        name="inproj",
    )(x, sh, sc, g, w_in_bf, cos, sa, sb)


def _attn_kernel(nseg, lam_init, q_ref, lam_ref, gsub_ref, *refs):
    kv = refs[:2 * nseg]
    o_ref = refs[2 * nseg]
    q = q_ref[0].astype(F32)
    lane = lax.broadcasted_iota(jnp.int32, q.shape, 1)
    qm = [jnp.where(lane < HEAD_D, q, 0.0).astype(BF16), jnp.where(lane >= HEAD_D, q, 0.0).astype(BF16)]
    lv = lam_ref[...]
    lam = (jnp.exp(jnp.sum(lv[0:1] * lv[1:2], axis=-1, keepdims=True))
           - jnp.exp(jnp.sum(lv[2:3] * lv[3:4], axis=-1, keepdims=True)) + lam_init)
    nt = (((1,), (1,)), ((), ()))
    probs = []
    for m in range(2):
        s = [lax.dot_general(qm[m], kv[2 * i][0], nt, preferred_element_type=F32) for i in range(nseg)]
        mx = functools.reduce(jnp.maximum, [jnp.max(si, axis=-1, keepdims=True) for si in s])
        p = [jnp.exp(si - mx) for si in s]
        den = functools.reduce(jnp.add, [jnp.sum(pi, axis=-1, keepdims=True) for pi in p])
        probs.append((p, den))
    r0 = 1.0 / probs[0][1]
    r1 = lam / probs[1][1]
    o = None
    for i in range(nseg):
        a = (probs[0][0][i] * r0 - probs[1][0][i] * r1).astype(BF16)
        oi = jnp.dot(a, kv[2 * i + 1][0], preferred_element_type=F32)
        o = oi if o is None else o + oi
    ms = jnp.mean(o * o, axis=-1, keepdims=True)
    o_ref[0] = ((o * lax.rsqrt(ms + EPS)) * gsub_ref[...] * (1.0 - lam_init)).astype(BF16)


def diff_attention(q, segs, lam_vec, g_sub, lam_init, tq=256):
    b, n, _ = q.shape
    in_specs = [
        pl.BlockSpec((1, tq, HEAD_W), lambda bi, h, i: (bi, i, h)),
        pl.BlockSpec((4, HEAD_D), lambda bi, h, i: (0, 0)),
        pl.BlockSpec((1, HEAD_W), lambda bi, h, i: (0, 0)),
    ]
    args = [q, lam_vec, g_sub]
    for kk, vv in segs:
        tk = kk.shape[1]
        in_specs += [pl.BlockSpec((1, tk, HEAD_W), lambda bi, h, i: (bi, 0, h))] * 2
        args += [kk, vv]
    return pl.pallas_call(
        functools.partial(_attn_kernel, len(segs), lam_init),
        grid=(b, N_HEADS, n // tq),
        in_specs=in_specs,
        out_specs=pl.BlockSpec((1, tq, HEAD_W), lambda bi, h, i: (bi, i, h)),
        out_shape=jax.ShapeDtypeStruct((b, n, ATTN_W), BF16),
        compiler_params=_params(("arbitrary", "arbitrary", "arbitrary")),
        name="diff_attention",
    )(*args)


def _conv_kernel(n, rows, z_ref, w_ref, b_ref, lg_ref, lb_ref, o_ref, zp_ref, y_ref):
    zp_ref[0:CONV_PAD, :] = jnp.zeros((CONV_PAD, CONV_W), F32)
    zp_ref[CONV_PAD + n:2 * CONV_PAD + n, :] = jnp.zeros((CONV_PAD, CONV_W), F32)
    zp_ref[CONV_PAD:CONV_PAD + n, :] = z_ref[0]
    first = CONV_PAD - CONV_K // 2

    def chunk(i, carry):
        r0 = pl.multiple_of(i * rows, rows)
        for c in range(CONV_W // 128):
            lanes = slice(c * 128, (c + 1) * 128)
            win = zp_ref[pl.ds(r0, rows + 2 * CONV_PAD), lanes]
            acc = jnp.zeros((rows, 128), F32)
            for k in range(CONV_K):
                acc = acc + win[first + k:first + k + rows, :] * w_ref[k:k + 1, lanes]
            y_ref[pl.ds(r0, rows), lanes] = acc + b_ref[:, lanes]
        return carry

    lax.fori_loop(0, n // rows, chunk, 0)
    y = y_ref[...]
    mu = jnp.mean(y, axis=-1, keepdims=True)
    d = y - mu
    var = jnp.mean(d * d, axis=-1, keepdims=True)
    yn = d * lax.rsqrt(var + EPS) * lg_ref[...] + lb_ref[...]
    o_ref[0] = (yn * jax.nn.sigmoid(yn)).astype(BF16)


def conformer_conv(z, w, bias, ln_g, ln_b):
    b, n, _ = z.shape
    vec = pl.BlockSpec((1, CONV_W), lambda bi: (0, 0))
    return pl.pallas_call(
        functools.partial(_conv_kernel, n, 64),
        grid=(b,),
        in_specs=[
            pl.BlockSpec((1, n, CONV_W), lambda bi: (bi, 0, 0)),
            pl.BlockSpec((CONV_K, CONV_W), lambda bi: (0, 0)),
            vec, vec, vec,
        ],
        out_specs=pl.BlockSpec((1, n, CONV_W), lambda bi: (bi, 0, 0)),
        out_shape=jax.ShapeDtypeStruct((b, n, CONV_W), BF16),
        scratch_shapes=[pltpu.VMEM((n + 2 * CONV_PAD, CONV_W), F32), pltpu.VMEM((n, CONV_W), F32)],
        compiler_params=_params(("arbitrary",)),
        name="conformer_conv",
    )(z, w, bias.reshape(1, CONV_W), ln_g.reshape(1, CONV_W), ln_b.reshape(1, CONV_W))


def _fourier_kernel(n, u_ref, wc_ref, cs_ref, o_ref, pq_ref):
    @pl.when(pl.program_id(1) == 0)
    def _():
        pq = jnp.dot(u_ref[0], wc_ref[...], preferred_element_type=F32)
        pq_ref[0:n, :] = pq[:, :FOUR_W].astype(BF16)
        pq_ref[n:2 * n, :] = pq[:, FOUR_W:].astype(BF16)

    o_ref[0] = jnp.dot(cs_ref[...], pq_ref[...], preferred_element_type=F32).astype(BF16)


def fourier_mix(u, wc, cs, tm=256):
    b, n, _ = u.shape
    tm = min(tm, n)
    return pl.pallas_call(
        functools.partial(_fourier_kernel, n),
        grid=(b, n // tm),
        in_specs=[
            pl.BlockSpec((1, n, FOUR_W), lambda bi, i: (bi, 0, 0)),
            pl.BlockSpec((FOUR_W, 2 * FOUR_W), lambda bi, i: (0, 0)),
            pl.BlockSpec((tm, 2 * n), lambda bi, i: (i, 0)),
        ],
        out_specs=pl.BlockSpec((1, tm, FOUR_W), lambda bi, i: (bi, i, 0)),
        out_shape=jax.ShapeDtypeStruct((b, n, FOUR_W), BF16),
        scratch_shapes=[pltpu.VMEM((2 * n, FOUR_W), BF16)],
        compiler_params=_params(("arbitrary", "arbitrary")),
        name="fourier_mix",
    )(u, wc, cs)


def dft_tables(n):
    j = jnp.arange(n, dtype=jnp.int32)
    ang = (2.0 * math.pi / n) * ((j[:, None] * j[None, :]) % n).astype(F32)
    cs = jnp.concatenate([jnp.cos(ang), -jnp.sin(ang)], axis=1) * (n ** -0.5)
    c = jnp.arange(FOUR_HD, dtype=jnp.int32)
    angc = (2.0 * math.pi / FOUR_HD) * ((c[:, None] * c[None, :]) % FOUR_HD).astype(F32)
    eye = jnp.eye(FOUR_HEADS, dtype=F32)
    wc = jnp.concatenate([jnp.kron(eye, jnp.cos(angc)), jnp.kron(eye, jnp.sin(angc))], axis=1) * (FOUR_HD ** -0.5)
    return cs.astype(BF16), wc.astype(BF16)


def _outproj_kernel(a_ref, c_ref, f_ref, w_ref, x_ref, gt_ref, g2_ref, sh_ref, sc_ref, wr_ref,
                    x1_ref, hp_ref, aff_ref):
    y = jnp.dot(a_ref[0], w_ref[0:ATTN_W, :], preferred_element_type=F32)
    y = y + jnp.dot(c_ref[0], w_ref[ATTN_W:ATTN_W + CONV_W, :], preferred_element_type=F32)
    y = y + jnp.dot(f_ref[0], w_ref[ATTN_W + CONV_W:, :], preferred_element_type=F32)
    x1 = x_ref[0] + gt_ref[0] * y
    x1_ref[0] = x1
    hb = _rms_mod(x1, g2_ref[...], sc_ref[0], sh_ref[0]).astype(BF16)
    bits = pltpu.bitcast(hb.astype(F32), jnp.uint32)
    hp_ref[0] = (bits[:, :HALF_D] >> 16) | bits[:, HALF_D:]
    logits = jnp.dot(hb, wr_ref[...], preferred_element_type=F32)
    lane = lax.broadcasted_iota(jnp.int32, logits.shape, 1)
    logits = jnp.where(lane < N_EXPERTS, logits, -1e30)
    ex = jnp.exp(logits - jnp.max(logits, axis=-1, keepdims=True))
    aff_ref[0] = ex / jnp.sum(ex, axis=-1, keepdims=True)


def outproj_router(attn, conv, four, w_out_bf, x, gt1, g2, sh2, sc2, wr_pad, tm=256):
    b, n, _ = x.shape
    row = lambda bi, i: (bi, i, 0)
    vec = lambda bi, i: (bi, 0, 0)
    return pl.pallas_call(
        _outproj_kernel,
        grid=(b, n // tm),
        in_specs=[
            pl.BlockSpec((1, tm, ATTN_W), row),
            pl.BlockSpec((1, tm, CONV_W), row),
            pl.BlockSpec((1, tm, FOUR_W), row),
            _resident((D_MODEL, D_MODEL), lambda bi, i: (0, 0)),
            pl.BlockSpec((1, tm, D_MODEL), row),
            pl.BlockSpec((1, 1, D_MODEL), vec),
            pl.BlockSpec((1, D_MODEL), lambda bi, i: (0, 0)),
            pl.BlockSpec((1, 1, D_MODEL), vec),
            pl.BlockSpec((1, 1, D_MODEL), vec),
            pl.BlockSpec((D_MODEL, ROUTER_LANES), lambda bi, i: (0, 0)),
        ],
        out_specs=[
            pl.BlockSpec((1, tm, D_MODEL), row),
            pl.BlockSpec((1, tm, HALF_D), row),
            pl.BlockSpec((1, tm, ROUTER_LANES), row),
        ],
        out_shape=[
            jax.ShapeDtypeStruct((b, n, D_MODEL), F32),
            jax.ShapeDtypeStruct((b, n, HALF_D), jnp.uint32),
            jax.ShapeDtypeStruct((b, n, ROUTER_LANES), F32),
        ],
        compiler_params=_params(("arbitrary", "arbitrary")),
        name="outproj_router",
    )(attn, conv, four, w_out_bf, x, gt1, g2, sh2, sc2, wr_pad)


def _ffn_kernel(group, cap, idx_ref, src_ref, wg_ref, wu_ref, wd_ref, o_ref, xs_ref):
    e = pl.program_id(0)
    bi = pl.program_id(1)
    base = (e * pl.num_programs(1) + bi) * (group * cap)
    for g in range(group):
        def gather_row(p, carry, g=g):
            tok = idx_ref[base + g * cap + p]
            xs_ref[pl.ds(g * cap + p, 1), :] = src_ref[g, pl.ds(tok, 1), :]
            return carry
        lax.fori_loop(0, cap, gather_row, 0)
    w = xs_ref[...]
    lo = pltpu.bitcast(w << 16, F32).astype(BF16)
    hi = pltpu.bitcast(w & jnp.uint32(0xFFFF0000), F32).astype(BF16)

    def proj(w_ref):
        return (jnp.dot(lo, w_ref[0, :HALF_D, :], preferred_element_type=F32)
                + jnp.dot(hi, w_ref[0, HALF_D:, :], preferred_element_type=F32))

    gate = proj(wg_ref)
    hid = (gate * jax.nn.sigmoid(gate) * proj(wu_ref)).astype(BF16)
    y = jnp.dot(hid, wd_ref[0], preferred_element_type=F32)
    for g in range(group):
        o_ref[g, 0] = y[g * cap:(g + 1) * cap, :]


def expert_ffn(hp, idx_flat, wg, wu, wd, cap, group):
    b, n, _ = hp.shape
    nb = b // group
    grid_spec = pltpu.PrefetchScalarGridSpec(
        num_scalar_prefetch=1,
        grid=(N_EXPERTS, nb),
        in_specs=[
            pl.BlockSpec((group, n, HALF_D), lambda e, bi, idx: (bi, 0, 0)),
            pl.BlockSpec((1, D_MODEL, EXPERT_FF), lambda e, bi, idx: (e, 0, 0)),
            pl.BlockSpec((1, D_MODEL, EXPERT_FF), lambda e, bi, idx: (e, 0, 0)),
            pl.BlockSpec((1, EXPERT_FF, D_MODEL), lambda e, bi, idx: (e, 0, 0)),
        ],
        out_specs=pl.BlockSpec((group, 1, cap, D_MODEL), lambda e, bi, idx: (bi, e, 0, 0)),
        scratch_shapes=[pltpu.VMEM((group * cap, HALF_D), jnp.uint32)],
    )
    return pl.pallas_call(
        functools.partial(_ffn_kernel, group, cap),
        grid_spec=grid_spec,
        out_shape=jax.ShapeDtypeStruct((b, N_EXPERTS, cap, D_MODEL), F32),
        compiler_params=_params(("arbitrary", "arbitrary")),
        name="expert_ffn",
    )(idx_flat, hp, wg, wu, wd)


def _combine_kernel(x_ref, gt_ref, ys_ref, idx_ref, gate_ref, o_ref, acc_ref):
    e = pl.program_id(2)

    @pl.when(e == 0)
    def _():
        acc_ref[...] = jnp.zeros(acc_ref.shape, F32)

    n = acc_ref.shape[0]
    cap = idx_ref.shape[-1]
    tok = lax.broadcasted_iota(jnp.int32, (n, cap), 0)
    onehot = jnp.where(tok == idx_ref[0, 0], 1.0, 0.0).astype(BF16)
    yg = (ys_ref[0, 0] * gate_ref[0, 0]).astype(BF16)
    acc_ref[...] += jnp.dot(onehot, yg, preferred_element_type=F32)

    @pl.when(e == pl.num_programs(2) - 1)
    def _():
        o_ref[0] = x_ref[0] + gt_ref[0] * acc_ref[...]


def moe_combine(x1, gt2, ys, idx, gates, td=512):
    b, n, _ = x1.shape
    cap = idx.shape[-1]
    return pl.pallas_call(
        _combine_kernel,
        grid=(b, D_MODEL // td, N_EXPERTS),
        in_specs=[
            pl.BlockSpec((1, n, td), lambda bi, d, e: (bi, 0, d)),
            pl.BlockSpec((1, 1, td), lambda bi, d, e: (bi, 0, d)),
            pl.BlockSpec((1, 1, cap, td), lambda bi, d, e: (bi, e, 0, d)),
            pl.BlockSpec((1, 1, 1, cap), lambda bi, d, e: (bi, e, 0, 0)),
            pl.BlockSpec((1, 1, cap, 1), lambda bi, d, e: (bi, e, 0, 0)),
        ],
        out_specs=pl.BlockSpec((1, n, td), lambda bi, d, e: (bi, 0, d)),
        out_shape=jax.ShapeDtypeStruct((b, n, D_MODEL), F32),
        scratch_shapes=[pltpu.VMEM((n, td), F32)],
        compiler_params=_params(("arbitrary", "arbitrary", "arbitrary")),
        name="moe_combine",
    )(x1, gt2, ys, idx.reshape(b, N_EXPERTS, 1, cap), gates.reshape(b, N_EXPERTS, cap, 1))


def _final_norm_kernel(x_ref, g_ref, o_ref):
    x = x_ref[0]
    ms = jnp.mean(x * x, axis=-1, keepdims=True)
    o_ref[0] = x * lax.rsqrt(ms + EPS) * g_ref[...]


def final_norm(x, g, tm=512):
    b, n, _ = x.shape
    return pl.pallas_call(
        _final_norm_kernel,
        grid=(b, n // tm),
        in_specs=[pl.BlockSpec((1, tm, D_MODEL), lambda bi, i: (bi, i, 0)),
                  pl.BlockSpec((1, D_MODEL), lambda bi, i: (0, 0))],
        out_specs=pl.BlockSpec((1, tm, D_MODEL), lambda bi, i: (bi, i, 0)),
        out_shape=jax.ShapeDtypeStruct(x.shape, F32),
        compiler_params=_params(("arbitrary", "arbitrary")),
        name="final_norm",
    )(x, g.reshape(1, D_MODEL))


def rope_tables(n):
    pos = jnp.arange(n, dtype=jnp.int32)
    rc = jnp.stack([pos // GRID_W, pos % GRID_W], axis=-1).astype(F32)
    inv_freq = ROPE_THETA ** (-jnp.arange(ROPE_FREQS, dtype=F32) / ROPE_FREQS)
    ang = rc[:, :, None, None] * inv_freq
    ang = jnp.broadcast_to(ang, (n, 2, 2, ROPE_FREQS)).reshape(n, HEAD_D)
    ang = jnp.concatenate([ang, ang], axis=-1)
    first_half = (jnp.arange(HEAD_W) % (2 * ROPE_FREQS)) < ROPE_FREQS
    sin = jnp.sin(ang)
    return jnp.cos(ang), jnp.where(first_half, -sin, 0.0), jnp.where(first_half, 0.0, sin)


def route_and_ffn(hp, aff, wg, wu, wd, group):
    b, n, _ = hp.shape
    cap = (CAPACITY_FACTOR * n) // N_EXPERTS
    gates, idx = lax.top_k(jnp.swapaxes(aff[..., :N_EXPERTS], 1, 2), cap)
    idx_flat = jnp.swapaxes(idx, 0, 1).reshape(-1).astype(jnp.int32)
    ys = expert_ffn(hp, idx_flat, wg, wu, wd, cap, group)
    return ys, idx.astype(jnp.int32), gates


def kernel(x, c, ctx, c_ctx, w_ada, b_ada, g_norm1, w_in, diff_lambda, g_sub, conv_w, conv_b, conv_ln_g,
           conv_ln_b, w_out, g_norm2, w_router, w_gate, w_up, w_down, g_final):
    b, n, d = x.shape
    nc = ctx.shape[1]
    pad_rows = (-(b + 1)) % 8
    cc = jnp.concatenate([c, c_ctx[None, :], jnp.zeros((pad_rows, d), F32)], axis=0)
    mod_all = ada_modulation(cc, w_ada, b_ada)

    cos, sa, sb = rope_tables(n)
    cos_c = jnp.ones((nc, HEAD_W), F32)
    zero_c = jnp.zeros((nc, HEAD_W), F32)
    cs_n, wc = dft_tables(n)
    cs_c, _ = dft_tables(nc)

    xc = ctx
    for l in range(DEPTH):
        last = l == DEPTH - 1
        lam_init = 0.8 - 0.6 * math.exp(-0.3 * l)
        mod = mod_all[l, :b].reshape(b, 1, 6 * d)
        modc = jnp.broadcast_to(mod_all[l, b].reshape(1, 1, 6 * d), (b, 1, 6 * d))
        sh1, sc1, gt1, sh2, sc2, gt2 = [mod[..., i * d:(i + 1) * d] for i in range(6)]
        csh1, csc1, cgt1, csh2, csc2, cgt2 = [modc[..., i * d:(i + 1) * d] for i in range(6)]
        g1 = g_norm1[l].reshape(1, d)
        g2 = g_norm2[l].reshape(1, d)
        w_in_bf = w_in[l].astype(BF16)
        w_out_bf = w_out[l].astype(BF16)
        wr_pad = jnp.pad(w_router[l], ((0, 0), (0, ROUTER_LANES - N_EXPERTS))).astype(BF16)
        wg, wu, wd = w_gate[l].astype(BF16), w_up[l].astype(BF16), w_down[l].astype(BF16)
        gs = g_sub[l].reshape(1, HEAD_W)
        conv_args = (conv_w[l], conv_b[l], conv_ln_g[l], conv_ln_b[l])

        q, k, v, z, uf = inproj(x, sh1, sc1, g1, w_in_bf, cos, sa, sb)
        qc, kc, vc, zc, ufc = inproj(xc, csh1, csc1, g1, w_in_bf, cos_c, zero_c, zero_c)

        attn = diff_attention(q, [(kc, vc), (k, v)], diff_lambda[l], gs, lam_init)
        conv = conformer_conv(z, *conv_args)
        four = fourier_mix(uf, wc, cs_n)
        x1, hp, aff = outproj_router(attn, conv, four, w_out_bf, x, gt1, g2, sh2, sc2, wr_pad)
        ys, idx, gates = route_and_ffn(hp, aff, wg, wu, wd, group=1)
        x = moe_combine(x1, gt2, ys, idx, gates)

        if not last:
            attn_c = diff_attention(qc, [(kc, vc)], diff_lambda[l], gs, lam_init)
            conv_c = conformer_conv(zc, *conv_args)
            four_c = fourier_mix(ufc, wc, cs_c)
            xc1, hpc, affc = outproj_router(attn_c, conv_c, four_c, w_out_bf, xc, cgt1, g2, csh2, csc2, wr_pad)
            ysc, idxc, gatesc = route_and_ffn(hpc, affc, wg, wu, wd, group=8)
            xc = moe_combine(xc1, cgt2, ysc, idxc, gatesc)
    return final_norm(x, g_final)
```

```python
import functools
import math

import jax
import jax.numpy as jnp
from jax import lax
from jax.experimental import pallas as pl
from jax.experimental.pallas import tpu as pltpu

D_MODEL = 2048
DEPTH = 4
GRID_W = 64
N_HEADS = 8
HEAD_D = 64
HEAD_W = 2 * HEAD_D
ATTN_W = N_HEADS * HEAD_W
QK_W = ATTN_W
ROPE_THETA = 10000.0
ROPE_FREQS = HEAD_D // 4
CONV_W = 512
CONV_K = 31
CONV_PAD = 16
FOUR_HEADS = 4
FOUR_HD = 128
FOUR_W = FOUR_HEADS * FOUR_HD
K0 = QK_W
V0 = 2 * QK_W
G0 = V0 + ATTN_W
F0 = G0 + 2 * CONV_W
IN_W = F0 + FOUR_W
N_EXPERTS = 16
EXPERT_FF = 1024
CAPACITY_FACTOR = 2
EPS = 1e-6
ROUTER_LANES = 128
HALF_D = D_MODEL // 2
GATHER_UNROLL = 8
ONES_ROWS = 16
ATTN_CHUNK = 256

VMEM_LIMIT = 56 * 1024 * 1024

F32 = jnp.float32
BF16 = jnp.bfloat16
NT_DIMS = (((1,), (1,)), ((), ()))


def _params(sem, vmem=VMEM_LIMIT):
    return pltpu.CompilerParams(dimension_semantics=sem, vmem_limit_bytes=vmem)


def _resident(shape, index_map):
    return pl.BlockSpec(shape, index_map, pipeline_mode=pl.Buffered(1))


def _ada_kernel(c_ref, w_ref, b_ref, o_ref):
    c = c_ref[...]
    a = (c * jax.nn.sigmoid(c)).astype(BF16)
    w = w_ref[0].astype(BF16)
    o_ref[0] = jnp.dot(a, w, preferred_element_type=F32) + b_ref[0]


def ada_modulation(cc, w_ada, b_ada):
    rows = cc.shape[0]
    tn = 1024
    return pl.pallas_call(
        _ada_kernel,
        grid=(DEPTH, 6 * D_MODEL // tn),
        in_specs=[
            pl.BlockSpec((rows, D_MODEL), lambda l, j: (0, 0)),
            pl.BlockSpec((1, D_MODEL, tn), lambda l, j: (l, 0, j)),
            pl.BlockSpec((1, 1, tn), lambda l, j: (l, 0, j)),
        ],
        out_specs=pl.BlockSpec((1, rows, tn), lambda l, j: (l, 0, j)),
        out_shape=jax.ShapeDtypeStruct((DEPTH, rows, 6 * D_MODEL), F32),
        compiler_params=_params(("arbitrary", "arbitrary")),
        name="ada_modulation",
    )(cc, w_ada, b_ada.reshape(DEPTH, 1, 6 * D_MODEL))


def _rms_mod(x, g, sc, sh):
    ms = jnp.mean(x * x, axis=-1, keepdims=True)
    return (x * lax.rsqrt(ms + EPS)) * g * (1.0 + sc) + sh


def _inproj_kernel(x_ref, sh_ref, sc_ref, g_ref, w_ref, wvt_ref, cos_ref, sa_ref, sb_ref, *refs):
    q_ref, k_ref, vt_ref, z_ref, f_ref = refs[-5:]
    hb = _rms_mod(x_ref[0], g_ref[...], sc_ref[0], sh_ref[0]).astype(BF16)
    cos = cos_ref[...]
    sa = sa_ref[...]
    sb = sb_ref[...]

    def rope_store(col0, out_ref, scale):
        t = jnp.dot(hb, w_ref[:, col0:col0 + QK_W], preferred_element_type=F32)
        for h in range(N_HEADS):
            th = t[:, h * HEAD_W:(h + 1) * HEAD_W]
            r = th * cos + pltpu.roll(th, HEAD_W - ROPE_FREQS, 1) * sa + pltpu.roll(th, ROPE_FREQS, 1) * sb
            out_ref[0, :, h * HEAD_W:(h + 1) * HEAD_W] = (r * scale).astype(BF16)

    rope_store(0, q_ref, HEAD_D ** -0.5 * math.log2(math.e))
    rope_store(K0, k_ref, 1.0)
    vt = lax.dot_general(wvt_ref[...], hb, NT_DIMS, preferred_element_type=F32).astype(BF16)
    for h in range(N_HEADS):
        vt_ref[0, h, 0, 0:HEAD_W, :] = vt[h * HEAD_W:(h + 1) * HEAD_W, :]
        vt_ref[0, h, 0, HEAD_W:HEAD_W + ONES_ROWS, :] = jnp.ones((ONES_ROWS, ATTN_CHUNK), BF16)
    a = jnp.dot(hb, w_ref[:, G0:G0 + CONV_W], preferred_element_type=F32)
    g = jnp.dot(hb, w_ref[:, G0 + CONV_W:F0], preferred_element_type=F32)
    z_ref[0] = a * jax.nn.sigmoid(g)
    f_ref[0] = jnp.dot(hb, w_ref[:, F0:IN_W], preferred_element_type=F32).astype(BF16)


def inproj(x, sh, sc, g, w_in_bf, wvt_bf, cos, sa, sb, total_keys, chunk_off, kv_prev=None):
    b, n, _ = x.shape
    tm = ATTN_CHUNK
    nchunk = total_keys // ATTN_CHUNK
    row = lambda bi, i: (bi, i, 0)
    vec = lambda bi, i: (bi, 0, 0)
    tab = lambda bi, i: (i, 0)
    out = lambda w, dt: jax.ShapeDtypeStruct((b, n, w), dt)
    in_specs = [
        pl.BlockSpec((1, tm, D_MODEL), row),
        pl.BlockSpec((1, 1, D_MODEL), vec),
        pl.BlockSpec((1, 1, D_MODEL), vec),
        pl.BlockSpec((1, D_MODEL), lambda bi, i: (0, 0)),
        _resident((D_MODEL, IN_W), lambda bi, i: (0, 0)),
        _resident((ATTN_W, D_MODEL), lambda bi, i: (0, 0)),
        pl.BlockSpec((tm, HEAD_W), tab),
        pl.BlockSpec((tm, HEAD_W), tab),
        pl.BlockSpec((tm, HEAD_W), tab),
    ]
    args = [x, sh, sc, g, w_in_bf, wvt_bf, cos, sa, sb]
    aliases = {}
    if kv_prev is not None:
        aliases = {len(args): 1, len(args) + 1: 2}
        in_specs += [pl.BlockSpec(memory_space=pl.ANY)] * 2
        args += list(kv_prev)
    return pl.pallas_call(
        _inproj_kernel,
        grid=(b, n // tm),
        in_specs=in_specs,
        out_specs=[
            pl.BlockSpec((1, tm, QK_W), row),
            pl.BlockSpec((1, tm, QK_W), lambda bi, i: (bi, i + chunk_off, 0)),
            pl.BlockSpec((1, N_HEADS, 1, HEAD_W + ONES_ROWS, tm), lambda bi, i: (bi, 0, i + chunk_off, 0, 0)),
            pl.BlockSpec((1, tm, CONV_W), row),
            pl.BlockSpec((1, tm, FOUR_W), row),
        ],
        out_shape=[out(QK_W, BF16),
                   jax.ShapeDtypeStruct((b, total_keys, QK_W), BF16),
                   jax.ShapeDtypeStruct((b, N_HEADS, nchunk, HEAD_W + ONES_ROWS, ATTN_CHUNK), BF16),
                   out(CONV_W, F32), out(FOUR_W, BF16)],
        input_output_aliases=aliases,
        compiler_params=_params(("arbitrary", "arbitrary")),
        name="inproj",
    )(*args)


def _attn_kernel(nchunk, lam_init, q_ref, lam_ref, gsub_ref, k_ref, vt_ref, o_ref):
    tq = q_ref.shape[1]
    ck = ATTN_CHUNK
    lv = lam_ref[...]
    lam = (jnp.exp(jnp.sum(lv[0:1] * lv[1:2], axis=-1, keepdims=True))
           - jnp.exp(jnp.sum(lv[2:3] * lv[3:4], axis=-1, keepdims=True)) + lam_init)
    chains = [(c, m) for c in range(tq // ck) for m in range(2)]

    def scores(c, m):
        q = q_ref[0, c * ck:(c + 1) * ck, :].astype(F32)
        lane = lax.broadcasted_iota(jnp.int32, q.shape, 1)
        qm = jnp.where((lane < HEAD_D) if m == 0 else (lane >= HEAD_D), q, 0.0).astype(BF16)
        return lax.dot_general(k_ref[0], qm, NT_DIMS, preferred_element_type=F32)

    def probs(s):
        mx = jnp.max(s, axis=0, keepdims=True)
        return jnp.exp2(s - mx).astype(BF16)

    def values(p):
        return functools.reduce(jnp.add, [jnp.dot(vt_ref[0, 0, j], p[j * ck:(j + 1) * ck], preferred_element_type=F32)
                                          for j in range(nchunk)])

    s_of, p_of, o_of = {}, {}, {}
    for t in range(len(chains) + 2):
        if t < len(chains):
            s_of[t] = scores(*chains[t])
        if 1 <= t <= len(chains):
            p_of[t - 1] = probs(s_of.pop(t - 1))
        if t >= 2:
            o_of[t - 2] = values(p_of.pop(t - 2))
    for c in range(tq // ck):
        o0, o1 = o_of[2 * c], o_of[2 * c + 1]
        ot = (o0[:HEAD_W] * (1.0 / o0[HEAD_W:HEAD_W + 1]) - o1[:HEAD_W] * (lam / o1[HEAD_W:HEAD_W + 1]))
        ms = jnp.mean(ot * ot, axis=0, keepdims=True)
        ot = (ot * lax.rsqrt(ms + EPS)) * gsub_ref[...] * (1.0 - lam_init)
        o_ref[0, c * ck:(c + 1) * ck, :] = ot.T.astype(BF16)


def diff_attention(q, k_all, vt_all, nkeys, lam_vec, g_sub_col, lam_init, tq):
    b, n, _ = q.shape
    nchunk = nkeys // ATTN_CHUNK
    return pl.pallas_call(
        functools.partial(_attn_kernel, nchunk, lam_init),
        grid=(b, N_HEADS, n // tq),
        in_specs=[
            pl.BlockSpec((1, tq, HEAD_W), lambda bi, h, i: (bi, i, h)),
            pl.BlockSpec((4, HEAD_D), lambda bi, h, i: (0, 0)),
            pl.BlockSpec((HEAD_W, 1), lambda bi, h, i: (0, 0)),
            pl.BlockSpec((1, nkeys, HEAD_W), lambda bi, h, i: (bi, 0, h)),
            pl.BlockSpec((1, 1, nchunk, HEAD_W + ONES_ROWS, ATTN_CHUNK), lambda bi, h, i: (bi, h, 0, 0, 0)),
        ],
        out_specs=pl.BlockSpec((1, tq, HEAD_W), lambda bi, h, i: (bi, i, h)),
        out_shape=jax.ShapeDtypeStruct((b, n, ATTN_W), BF16),
        compiler_params=_params(("arbitrary", "arbitrary", "arbitrary")),
        name="diff_attention",
    )(q, lam_vec, g_sub_col, k_all, vt_all)


def _conv_kernel(n, rows, z_ref, w_ref, b_ref, lg_ref, lb_ref, o_ref, zp_ref, y_ref):
    zp_ref[0:CONV_PAD, :] = jnp.zeros((CONV_PAD, CONV_W), F32)
    zp_ref[CONV_PAD + n:2 * CONV_PAD + n, :] = jnp.zeros((CONV_PAD, CONV_W), F32)
    zp_ref[CONV_PAD:CONV_PAD + n, :] = z_ref[0]
    first = CONV_PAD - CONV_K // 2

    def chunk(i, carry):
        r0 = pl.multiple_of(i * rows, rows)
        for c in range(CONV_W // 128):
            lanes = slice(c * 128, (c + 1) * 128)
            win = zp_ref[pl.ds(r0, rows + 2 * CONV_PAD), lanes]
            acc = jnp.zeros((rows, 128), F32)
            for k in range(CONV_K):
                acc = acc + win[first + k:first + k + rows, :] * w_ref[k:k + 1, lanes]
            y_ref[pl.ds(r0, rows), lanes] = acc + b_ref[:, lanes]
        return carry

    lax.fori_loop(0, n // rows, chunk, 0)
    y = y_ref[...]
    mu = jnp.mean(y, axis=-1, keepdims=True)
    d = y - mu
    var = jnp.mean(d * d, axis=-1, keepdims=True)
    yn = d * lax.rsqrt(var + EPS) * lg_ref[...] + lb_ref[...]
    o_ref[0] = (yn * jax.nn.sigmoid(yn)).astype(BF16)


def conformer_conv(z, w, bias, ln_g, ln_b):
    b, n, _ = z.shape
    vec = pl.BlockSpec((1, CONV_W), lambda bi: (0, 0))
    return pl.pallas_call(
        functools.partial(_conv_kernel, n, 64),
        grid=(b,),
        in_specs=[
            pl.BlockSpec((1, n, CONV_W), lambda bi: (bi, 0, 0)),
            pl.BlockSpec((CONV_K, CONV_W), lambda bi: (0, 0)),
            vec, vec, vec,
        ],
        out_specs=pl.BlockSpec((1, n, CONV_W), lambda bi: (bi, 0, 0)),
        out_shape=jax.ShapeDtypeStruct((b, n, CONV_W), BF16),
        scratch_shapes=[pltpu.VMEM((n + 2 * CONV_PAD, CONV_W), F32), pltpu.VMEM((n, CONV_W), F32)],
        compiler_params=_params(("arbitrary",)),
        name="conformer_conv",
    )(z, w, bias.reshape(1, CONV_W), ln_g.reshape(1, CONV_W), ln_b.reshape(1, CONV_W))


def _fourier_kernel(n, u_ref, wc_ref, cs_ref, o_ref, pq_ref):
    @pl.when(pl.program_id(1) == 0)
    def _():
        pq = jnp.dot(u_ref[0], wc_ref[...], preferred_element_type=F32)
        pq_ref[0:n, :] = pq[:, :FOUR_W].astype(BF16)
        pq_ref[n:2 * n, :] = pq[:, FOUR_W:].astype(BF16)

    o_ref[0] = jnp.dot(cs_ref[...], pq_ref[...], preferred_element_type=F32).astype(BF16)


def fourier_mix(u, wc, cs, tm=256):
    b, n, _ = u.shape
    tm = min(tm, n)
    return pl.pallas_call(
        functools.partial(_fourier_kernel, n),
        grid=(b, n // tm),
        in_specs=[
            pl.BlockSpec((1, n, FOUR_W), lambda bi, i: (bi, 0, 0)),
            pl.BlockSpec((FOUR_W, 2 * FOUR_W), lambda bi, i: (0, 0)),
            pl.BlockSpec((tm, 2 * n), lambda bi, i: (i, 0)),
        ],
        out_specs=pl.BlockSpec((1, tm, FOUR_W), lambda bi, i: (bi, i, 0)),
        out_shape=jax.ShapeDtypeStruct((b, n, FOUR_W), BF16),
        scratch_shapes=[pltpu.VMEM((2 * n, FOUR_W), BF16)],
        compiler_params=_params(("arbitrary", "arbitrary")),
        name="fourier_mix",
    )(u, wc, cs)


def dft_tables(n):
    j = jnp.arange(n, dtype=jnp.int32)
    ang = (2.0 * math.pi / n) * ((j[:, None] * j[None, :]) % n).astype(F32)
    cs = jnp.concatenate([jnp.cos(ang), -jnp.sin(ang)], axis=1) * (n ** -0.5)
    c = jnp.arange(FOUR_HD, dtype=jnp.int32)
    angc = (2.0 * math.pi / FOUR_HD) * ((c[:, None] * c[None, :]) % FOUR_HD).astype(F32)
    eye = jnp.eye(FOUR_HEADS, dtype=F32)
    wc = jnp.concatenate([jnp.kron(eye, jnp.cos(angc)), jnp.kron(eye, jnp.sin(angc))], axis=1) * (FOUR_HD ** -0.5)
    return cs.astype(BF16), wc.astype(BF16)


def _outproj_kernel(a_ref, c_ref, f_ref, w_ref, x_ref, gt_ref, g2_ref, sh_ref, sc_ref, wr_ref,
                    x1_ref, hp_ref, aff_ref):
    y = jnp.dot(a_ref[0], w_ref[0:ATTN_W, :], preferred_element_type=F32)
    y = y + jnp.dot(c_ref[0], w_ref[ATTN_W:ATTN_W + CONV_W, :], preferred_element_type=F32)
    y = y + jnp.dot(f_ref[0], w_ref[ATTN_W + CONV_W:, :], preferred_element_type=F32)
    x1 = x_ref[0] + gt_ref[0] * y
    x1_ref[0] = x1
    hb = _rms_mod(x1, g2_ref[...], sc_ref[0], sh_ref[0]).astype(BF16)
    bits = pltpu.bitcast(hb.astype(F32), jnp.uint32)
    hp_ref[0] = (bits[:, :HALF_D] >> 16) | bits[:, HALF_D:]
    logits = jnp.dot(hb, wr_ref[...], preferred_element_type=F32)
    lane = lax.broadcasted_iota(jnp.int32, logits.shape, 1)
    logits = jnp.where(lane < N_EXPERTS, logits, -1e30)
    ex = jnp.exp(logits - jnp.max(logits, axis=-1, keepdims=True))
    aff_ref[0] = ex / jnp.sum(ex, axis=-1, keepdims=True)


def outproj_router(attn, conv, four, w_out_bf, x, gt1, g2, sh2, sc2, wr_pad, tm=256):
    b, n, _ = x.shape
    row = lambda bi, i: (bi, i, 0)
    vec = lambda bi, i: (bi, 0, 0)
    return pl.pallas_call(
        _outproj_kernel,
        grid=(b, n // tm),
        in_specs=[
            pl.BlockSpec((1, tm, ATTN_W), row),
            pl.BlockSpec((1, tm, CONV_W), row),
            pl.BlockSpec((1, tm, FOUR_W), row),
            _resident((D_MODEL, D_MODEL), lambda bi, i: (0, 0)),
            pl.BlockSpec((1, tm, D_MODEL), row),
            pl.BlockSpec((1, 1, D_MODEL), vec),
            pl.BlockSpec((1, D_MODEL), lambda bi, i: (0, 0)),
            pl.BlockSpec((1, 1, D_MODEL), vec),
            pl.BlockSpec((1, 1, D_MODEL), vec),
            pl.BlockSpec((D_MODEL, ROUTER_LANES), lambda bi, i: (0, 0)),
        ],
        out_specs=[
            pl.BlockSpec((1, tm, D_MODEL), row),
            pl.BlockSpec((1, tm, HALF_D), row),
            pl.BlockSpec((1, tm, ROUTER_LANES), row),
        ],
        out_shape=[
            jax.ShapeDtypeStruct((b, n, D_MODEL), F32),
            jax.ShapeDtypeStruct((b, n, HALF_D), jnp.uint32),
            jax.ShapeDtypeStruct((b, n, ROUTER_LANES), F32),
        ],
        compiler_params=_params(("arbitrary", "arbitrary")),
        name="outproj_router",
    )(attn, conv, four, w_out_bf, x, gt1, g2, sh2, sc2, wr_pad)


def _prefix_count(x01, tri):
    carry = jnp.zeros((x01.shape[0], 1), F32)
    parts = []
    for c in range(x01.shape[1] // 128):
        loc = jnp.dot(x01[:, c * 128:(c + 1) * 128].astype(BF16), tri, preferred_element_type=F32)
        parts.append(loc + carry)
        carry = carry + loc[:, 127:128]
    return jnp.concatenate(parts, axis=1)


def _route_kernel(cap, aff_ref, idx_ref, gate_ref):
    aff = aff_ref[0]
    n = aff.shape[0]
    a = aff.T[0:N_EXPERTS, :]
    bits = pltpu.bitcast(a, jnp.int32)
    thr = jnp.zeros((N_EXPERTS, 1), jnp.int32)
    for bit in range(30, -1, -1):
        cand = thr | (1 << bit)
        cnt = jnp.sum(jnp.where(bits >= cand, 1.0, 0.0), axis=1, keepdims=True)
        thr = jnp.where(cnt >= cap, cand, thr)
    above = jnp.where(bits > thr, 1.0, 0.0)
    tied = jnp.where(bits == thr, 1.0, 0.0)
    r = lax.broadcasted_iota(jnp.int32, (128, 128), 0)
    c = lax.broadcasted_iota(jnp.int32, (128, 128), 1)
    tri = jnp.where(r <= c, 1.0, 0.0).astype(BF16)
    need = cap - jnp.sum(above, axis=1, keepdims=True)
    sel = above + tied * jnp.where(_prefix_count(tied, tri) <= need, 1.0, 0.0)
    slot_of = jnp.where(sel > 0.0, _prefix_count(sel, tri) - 1.0, -1.0)
    slot_t = jnp.concatenate([slot_of, jnp.full((ROUTER_LANES - N_EXPERTS, n), -1.0, F32)], axis=0).T
    slot = lax.broadcasted_iota(jnp.int32, (n, cap), 1).astype(F32)
    tok = lax.broadcasted_iota(jnp.int32, (n, cap), 0)
    for e in range(N_EXPERTS):
        hit = slot_t[:, e:e + 1] == slot
        idx_ref[0, e:e + 1, :] = jnp.sum(jnp.where(hit, tok, 0), axis=0, keepdims=True)
        gate_ref[0, e:e + 1, :] = jnp.sum(jnp.where(hit, aff[:, e:e + 1], 0.0), axis=0, keepdims=True)


def route(aff, cap):
    b, n, _ = aff.shape
    return pl.pallas_call(
        functools.partial(_route_kernel, cap),
        grid=(b,),
        in_specs=[pl.BlockSpec((1, n, ROUTER_LANES), lambda bi: (bi, 0, 0))],
        out_specs=[pl.BlockSpec((1, N_EXPERTS, cap), lambda bi: (bi, 0, 0))] * 2,
        out_shape=[jax.ShapeDtypeStruct((b, N_EXPERTS, cap), jnp.int32),
                   jax.ShapeDtypeStruct((b, N_EXPERTS, cap), F32)],
        compiler_params=_params(("arbitrary",)),
        name="route",
    )(aff)


def _ffn_kernel(group, cap, idx_ref, src_ref, wg_ref, wu_ref, wd_ref, o_ref, xs_ref):
    e = pl.program_id(0)
    bi = pl.program_id(1)
    base = (e * pl.num_programs(1) + bi) * (group * cap)
    for g in range(group):
        def gather_row(p, carry, g=g):
            tok = idx_ref[base + g * cap + p]
            xs_ref[pl.ds(g * cap + p, 1), :] = src_ref[g, pl.ds(tok, 1), :]
            return carry
        lax.fori_loop(0, cap, gather_row, 0, unroll=GATHER_UNROLL)
    w = xs_ref[...]
    lo = pltpu.bitcast(w << 16, F32).astype(BF16)
    hi = pltpu.bitcast(w & jnp.uint32(0xFFFF0000), F32).astype(BF16)

    def proj(w_ref):
        return (jnp.dot(lo, w_ref[0, :HALF_D, :], preferred_element_type=F32)
                + jnp.dot(hi, w_ref[0, HALF_D:, :], preferred_element_type=F32))

    gate = proj(wg_ref)
    hid = (gate * jax.nn.sigmoid(gate) * proj(wu_ref)).astype(BF16)
    y = jnp.dot(hid, wd_ref[0], preferred_element_type=F32)
    for g in range(group):
        o_ref[g, 0] = y[g * cap:(g + 1) * cap, :]


def expert_ffn(hp, idx_flat, wg, wu, wd, cap, group):
    b, n, _ = hp.shape
    nb = b // group
    grid_spec = pltpu.PrefetchScalarGridSpec(
        num_scalar_prefetch=1,
        grid=(N_EXPERTS, nb),
        in_specs=[
            pl.BlockSpec((group, n, HALF_D), lambda e, bi, idx: (bi, 0, 0)),
            pl.BlockSpec((1, D_MODEL, EXPERT_FF), lambda e, bi, idx: (e, 0, 0)),
            pl.BlockSpec((1, D_MODEL, EXPERT_FF), lambda e, bi, idx: (e, 0, 0)),
            pl.BlockSpec((1, EXPERT_FF, D_MODEL), lambda e, bi, idx: (e, 0, 0)),
        ],
        out_specs=pl.BlockSpec((group, 1, cap, D_MODEL), lambda e, bi, idx: (bi, e, 0, 0)),
        scratch_shapes=[pltpu.VMEM((group * cap, HALF_D), jnp.uint32)],
    )
    return pl.pallas_call(
        functools.partial(_ffn_kernel, group, cap),
        grid_spec=grid_spec,
        out_shape=jax.ShapeDtypeStruct((b, N_EXPERTS, cap, D_MODEL), F32),
        compiler_params=_params(("arbitrary", "arbitrary")),
        name="expert_ffn",
    )(idx_flat, hp, wg, wu, wd)


def _combine_kernel(ek, x_ref, gt_ref, ys_ref, idx_ref, gate_ref, o_ref, acc_ref):
    e = pl.program_id(2)

    @pl.when(e == 0)
    def _():
        acc_ref[...] = jnp.zeros(acc_ref.shape, F32)

    n = acc_ref.shape[0]
    cap = idx_ref.shape[-1]
    tok = lax.broadcasted_iota(jnp.int32, (n, cap), 0)
    for j in range(ek):
        onehot = jnp.where(tok == idx_ref[0, j], 1.0, 0.0).astype(BF16)
        yg = (ys_ref[0, j] * gate_ref[0, j]).astype(BF16)
        acc_ref[...] += jnp.dot(onehot, yg, preferred_element_type=F32)

    @pl.when(e == pl.num_programs(2) - 1)
    def _():
        o_ref[0] = x_ref[0] + gt_ref[0] * acc_ref[...]


def moe_combine(x1, gt2, ys, idx, gates, td, ek):
    b, n, _ = x1.shape
    cap = idx.shape[-1]
    return pl.pallas_call(
        functools.partial(_combine_kernel, ek),
        grid=(b, D_MODEL // td, N_EXPERTS // ek),
        in_specs=[
            pl.BlockSpec((1, n, td), lambda bi, d, e: (bi, 0, d)),
            pl.BlockSpec((1, 1, td), lambda bi, d, e: (bi, 0, d)),
            pl.BlockSpec((1, ek, cap, td), lambda bi, d, e: (bi, e, 0, d)),
            pl.BlockSpec((1, ek, 1, cap), lambda bi, d, e: (bi, e, 0, 0)),
            pl.BlockSpec((1, ek, cap, 1), lambda bi, d, e: (bi, e, 0, 0)),
        ],
        out_specs=pl.BlockSpec((1, n, td), lambda bi, d, e: (bi, 0, d)),
        out_shape=jax.ShapeDtypeStruct((b, n, D_MODEL), F32),
        scratch_shapes=[pltpu.VMEM((n, td), F32)],
        compiler_params=_params(("arbitrary", "arbitrary", "arbitrary")),
        name="moe_combine",
    )(x1, gt2, ys, idx.reshape(b, N_EXPERTS, 1, cap), gates.reshape(b, N_EXPERTS, cap, 1))


def _final_norm_kernel(x_ref, g_ref, o_ref):
    x = x_ref[0]
    ms = jnp.mean(x * x, axis=-1, keepdims=True)
    o_ref[0] = x * lax.rsqrt(ms + EPS) * g_ref[...]


def final_norm(x, g, tm=512):
    b, n, _ = x.shape
    return pl.pallas_call(
        _final_norm_kernel,
        grid=(b, n // tm),
        in_specs=[pl.BlockSpec((1, tm, D_MODEL), lambda bi, i: (bi, i, 0)),
                  pl.BlockSpec((1, D_MODEL), lambda bi, i: (0, 0))],
        out_specs=pl.BlockSpec((1, tm, D_MODEL), lambda bi, i: (bi, i, 0)),
        out_shape=jax.ShapeDtypeStruct(x.shape, F32),
        compiler_params=_params(("arbitrary", "arbitrary")),
        name="final_norm",
    )(x, g.reshape(1, D_MODEL))


def rope_tables(n):
    pos = jnp.arange(n, dtype=jnp.int32)
    rc = jnp.stack([pos // GRID_W, pos % GRID_W], axis=-1).astype(F32)
    inv_freq = ROPE_THETA ** (-jnp.arange(ROPE_FREQS, dtype=F32) / ROPE_FREQS)
    ang = rc[:, :, None, None] * inv_freq
    ang = jnp.broadcast_to(ang, (n, 2, 2, ROPE_FREQS)).reshape(n, HEAD_D)
    ang = jnp.concatenate([ang, ang], axis=-1)
    first_half = (jnp.arange(HEAD_W) % (2 * ROPE_FREQS)) < ROPE_FREQS
    sin = jnp.sin(ang)
    return jnp.cos(ang), jnp.where(first_half, -sin, 0.0), jnp.where(first_half, 0.0, sin)


def route_and_ffn(hp, aff, wg, wu, wd, group):
    b, n, _ = hp.shape
    cap = (CAPACITY_FACTOR * n) // N_EXPERTS
    idx, gates = route(aff, cap)
    idx_flat = jnp.swapaxes(idx, 0, 1).reshape(-1)
    ys = expert_ffn(hp, idx_flat, wg, wu, wd, cap, group)
    return ys, idx, gates


def kernel(x, c, ctx, c_ctx, w_ada, b_ada, g_norm1, w_in, diff_lambda, g_sub, conv_w, conv_b, conv_ln_g,
           conv_ln_b, w_out, g_norm2, w_router, w_gate, w_up, w_down, g_final):
    b, n, d = x.shape
    nc = ctx.shape[1]
    pad_rows = (-(b + 1)) % 8
    cc = jnp.concatenate([c, c_ctx[None, :], jnp.zeros((pad_rows, d), F32)], axis=0)
    mod_all = ada_modulation(cc, w_ada, b_ada)

    cos, sa, sb = rope_tables(n)
    cos_c = jnp.ones((nc, HEAD_W), F32)
    zero_c = jnp.zeros((nc, HEAD_W), F32)
    cs_n, wc = dft_tables(n)
    cs_c, _ = dft_tables(nc)

    xc = ctx
    for l in range(DEPTH):
        last = l == DEPTH - 1
        lam_init = 0.8 - 0.6 * math.exp(-0.3 * l)
        mod = mod_all[l, :b].reshape(b, 1, 6 * d)
        modc = jnp.broadcast_to(mod_all[l, b].reshape(1, 1, 6 * d), (b, 1, 6 * d))
        sh1, sc1, gt1, sh2, sc2, gt2 = [mod[..., i * d:(i + 1) * d] for i in range(6)]
        csh1, csc1, cgt1, csh2, csc2, cgt2 = [modc[..., i * d:(i + 1) * d] for i in range(6)]
        g1 = g_norm1[l].reshape(1, d)
        g2 = g_norm2[l].reshape(1, d)
        w_in_bf = w_in[l].astype(BF16)
        w_out_bf = w_out[l].astype(BF16)
        wr_pad = jnp.pad(w_router[l], ((0, 0), (0, ROUTER_LANES - N_EXPERTS))).astype(BF16)
        wg, wu, wd = w_gate[l].astype(BF16), w_up[l].astype(BF16), w_down[l].astype(BF16)
        wvt_bf = w_in[l][:, V0:G0].T.astype(BF16)
        gs = g_sub[l].reshape(HEAD_W, 1)
        conv_args = (conv_w[l], conv_b[l], conv_ln_g[l], conv_ln_b[l])

        qc, k_all, vt_all, zc, ufc = inproj(xc, csh1, csc1, g1, w_in_bf, wvt_bf, cos_c, zero_c, zero_c,
                                            total_keys=nc + n, chunk_off=0)
        q, k_all, vt_all, z, uf = inproj(x, sh1, sc1, g1, w_in_bf, wvt_bf, cos, sa, sb,
                                         total_keys=nc + n, chunk_off=nc // ATTN_CHUNK, kv_prev=(k_all, vt_all))

        attn = diff_attention(q, k_all, vt_all, nc + n, diff_lambda[l], gs, lam_init, tq=2 * ATTN_CHUNK)
        conv = conformer_conv(z, *conv_args)
        four = fourier_mix(uf, wc, cs_n)
        x1, hp, aff = outproj_router(attn, conv, four, w_out_bf, x, gt1, g2, sh2, sc2, wr_pad)
        ys, idx, gates = route_and_ffn(hp, aff, wg, wu, wd, group=1)
        x = moe_combine(x1, gt2, ys, idx, gates, td=512, ek=1)

        if not last:
            attn_c = diff_attention(qc, k_all, vt_all, nc, diff_lambda[l], gs, lam_init, tq=ATTN_CHUNK)
            conv_c = conformer_conv(zc, *conv_args)
            four_c = fourier_mix(ufc, wc, cs_c)
            xc1, hpc, affc = outproj_router(attn_c, conv_c, four_c, w_out_bf, xc, cgt1, g2, csh2, csc2, wr_pad)
            ysc, idxc, gatesc = route_and_ffn(hpc, affc, wg, wu, wd, group=8)
            xc = moe_combine(xc1, cgt2, ysc, idxc, gatesc, td=1024, ek=N_EXPERTS)
    return final_norm(x, g_final)
```

```python
import functools
import math

import jax
import jax.numpy as jnp
from jax import lax
from jax.experimental import pallas as pl
from jax.experimental.pallas import tpu as pltpu

D_MODEL = 2048
DEPTH = 4
GRID_W = 64
N_HEADS = 8
HEAD_D = 64
HEAD_W = 2 * HEAD_D
ATTN_W = N_HEADS * HEAD_W
QK_W = ATTN_W
ROPE_THETA = 10000.0
ROPE_FREQS = HEAD_D // 4
CONV_W = 512
CONV_K = 31
CONV_PAD = 16
FOUR_HEADS = 4
FOUR_HD = 128
FOUR_W = FOUR_HEADS * FOUR_HD
K0 = QK_W
V0 = 2 * QK_W
G0 = V0 + ATTN_W
F0 = G0 + 2 * CONV_W
IN_W = F0 + FOUR_W
N_EXPERTS = 16
EXPERT_FF = 1024
CAPACITY_FACTOR = 2
EPS = 1e-6
ROUTER_LANES = 128
HALF_D = D_MODEL // 2
GATHER_UNROLL = 8
ONES_ROWS = 16
ATTN_CHUNK = 256

VMEM_LIMIT = 56 * 1024 * 1024

F32 = jnp.float32
BF16 = jnp.bfloat16
NT_DIMS = (((1,), (1,)), ((), ()))


def _params(sem, vmem=VMEM_LIMIT):
    return pltpu.CompilerParams(dimension_semantics=sem, vmem_limit_bytes=vmem)


def _resident(shape, index_map):
    return pl.BlockSpec(shape, index_map, pipeline_mode=pl.Buffered(1))


def _ada_kernel(c_ref, w_ref, b_ref, o_ref):
    c = c_ref[...]
    a = (c * jax.nn.sigmoid(c)).astype(BF16)
    w = w_ref[0].astype(BF16)
    o_ref[0] = jnp.dot(a, w, preferred_element_type=F32) + b_ref[0]


def ada_modulation(cc, w_ada, b_ada):
    rows = cc.shape[0]
    tn = 1024
    return pl.pallas_call(
        _ada_kernel,
        grid=(DEPTH, 6 * D_MODEL // tn),
        in_specs=[
            pl.BlockSpec((rows, D_MODEL), lambda l, j: (0, 0)),
            pl.BlockSpec((1, D_MODEL, tn), lambda l, j: (l, 0, j)),
            pl.BlockSpec((1, 1, tn), lambda l, j: (l, 0, j)),
        ],
        out_specs=pl.BlockSpec((1, rows, tn), lambda l, j: (l, 0, j)),
        out_shape=jax.ShapeDtypeStruct((DEPTH, rows, 6 * D_MODEL), F32),
        compiler_params=_params(("arbitrary", "arbitrary")),
        name="ada_modulation",
    )(cc, w_ada, b_ada.reshape(DEPTH, 1, 6 * D_MODEL))


def _rms_mod(x, g, sc, sh):
    ms = jnp.mean(x * x, axis=-1, keepdims=True)
    return (x * lax.rsqrt(ms + EPS)) * g * (1.0 + sc) + sh


def _inproj_kernel(nprefix, *refs):
    if nprefix == 0:
        _inproj_body(*refs)
        return
    ins, (kc_ref, vtc_ref), outs = refs[:9], refs[9:11], refs[11:]
    i = pl.program_id(1)

    @pl.when(i < nprefix)
    def _():
        outs[1][...] = kc_ref[...]
        outs[2][...] = vtc_ref[...]

    @pl.when(i >= nprefix)
    def _():
        _inproj_body(*ins, *outs)


def _inproj_body(x_ref, sh_ref, sc_ref, g_ref, w_ref, wvt_ref, cos_ref, sa_ref, sb_ref,
                 q_ref, k_ref, vt_ref, z_ref, f_ref):
    hb = _rms_mod(x_ref[0], g_ref[...], sc_ref[0], sh_ref[0]).astype(BF16)
    cos = cos_ref[...]
    sa = sa_ref[...]
    sb = sb_ref[...]

    def rope_store(col0, out_ref, scale):
        t = jnp.dot(hb, w_ref[:, col0:col0 + QK_W], preferred_element_type=F32)
        for h in range(N_HEADS):
            th = t[:, h * HEAD_W:(h + 1) * HEAD_W]
            r = th * cos + pltpu.roll(th, HEAD_W - ROPE_FREQS, 1) * sa + pltpu.roll(th, ROPE_FREQS, 1) * sb
            out_ref[0, :, h * HEAD_W:(h + 1) * HEAD_W] = (r * scale).astype(BF16)

    rope_store(0, q_ref, HEAD_D ** -0.5 * math.log2(math.e))
    rope_store(K0, k_ref, 1.0)
    vt = lax.dot_general(wvt_ref[...], hb, NT_DIMS, preferred_element_type=F32).astype(BF16)
    for h in range(N_HEADS):
        vt_ref[0, h, 0, 0:HEAD_W, :] = vt[h * HEAD_W:(h + 1) * HEAD_W, :]
        vt_ref[0, h, 0, HEAD_W:HEAD_W + ONES_ROWS, :] = jnp.ones((ONES_ROWS, ATTN_CHUNK), BF16)
    a = jnp.dot(hb, w_ref[:, G0:G0 + CONV_W], preferred_element_type=F32)
    g = jnp.dot(hb, w_ref[:, G0 + CONV_W:F0], preferred_element_type=F32)
    z_ref[0] = a * jax.nn.sigmoid(g)
    f_ref[0] = jnp.dot(hb, w_ref[:, F0:IN_W], preferred_element_type=F32).astype(BF16)


def inproj(x, sh, sc, g, w_in_bf, wvt_bf, cos, sa, sb, kv_prefix=None):
    b, n, _ = x.shape
    tm = ATTN_CHUNK
    nprefix = 0 if kv_prefix is None else kv_prefix[0].shape[1] // ATTN_CHUNK
    nchunk = nprefix + n // ATTN_CHUNK
    step = lambda i: jnp.maximum(i - nprefix, 0)
    row = lambda bi, i: (bi, step(i), 0)
    vec = lambda bi, i: (bi, 0, 0)
    tab = lambda bi, i: (step(i), 0)
    out = lambda w, dt: jax.ShapeDtypeStruct((b, n, w), dt)
    in_specs = [
        pl.BlockSpec((1, tm, D_MODEL), row),
        pl.BlockSpec((1, 1, D_MODEL), vec),
        pl.BlockSpec((1, 1, D_MODEL), vec),
        pl.BlockSpec((1, D_MODEL), lambda bi, i: (0, 0)),
        _resident((D_MODEL, IN_W), lambda bi, i: (0, 0)),
        _resident((ATTN_W, D_MODEL), lambda bi, i: (0, 0)),
        pl.BlockSpec((tm, HEAD_W), tab),
        pl.BlockSpec((tm, HEAD_W), tab),
        pl.BlockSpec((tm, HEAD_W), tab),
    ]
    args = [x, sh, sc, g, w_in_bf, wvt_bf, cos, sa, sb]
    if nprefix:
        pre = lambda i: jnp.minimum(i, nprefix - 1)
        in_specs += [pl.BlockSpec((1, tm, QK_W), lambda bi, i: (bi, pre(i), 0)),
                     pl.BlockSpec((1, N_HEADS, 1, HEAD_W + ONES_ROWS, tm), lambda bi, i: (bi, 0, pre(i), 0, 0))]
        args += list(kv_prefix)
    return pl.pallas_call(
        functools.partial(_inproj_kernel, nprefix),
        grid=(b, nchunk),
        in_specs=in_specs,
        out_specs=[
            pl.BlockSpec((1, tm, QK_W), row),
            pl.BlockSpec((1, tm, QK_W), lambda bi, i: (bi, i, 0)),
            pl.BlockSpec((1, N_HEADS, 1, HEAD_W + ONES_ROWS, tm), lambda bi, i: (bi, 0, i, 0, 0)),
            pl.BlockSpec((1, tm, CONV_W), row),
            pl.BlockSpec((1, tm, FOUR_W), row),
        ],
        out_shape=[out(QK_W, BF16),
                   jax.ShapeDtypeStruct((b, nchunk * ATTN_CHUNK, QK_W), BF16),
                   jax.ShapeDtypeStruct((b, N_HEADS, nchunk, HEAD_W + ONES_ROWS, ATTN_CHUNK), BF16),
                   out(CONV_W, F32), out(FOUR_W, BF16)],
        compiler_params=_params(("arbitrary", "arbitrary")),
        name="inproj",
    )(*args)


def _attn_kernel(nchunk, lam_init, q_ref, lam_ref, gsub_ref, k_ref, vt_ref, o_ref):
    tq = q_ref.shape[1]
    ck = ATTN_CHUNK
    lv = lam_ref[...]
    lam = (jnp.exp(jnp.sum(lv[0:1] * lv[1:2], axis=-1, keepdims=True))
           - jnp.exp(jnp.sum(lv[2:3] * lv[3:4], axis=-1, keepdims=True)) + lam_init)
    chains = [(c, m) for c in range(tq // ck) for m in range(2)]

    def scores(c, m):
        q = q_ref[0, c * ck:(c + 1) * ck, :].astype(F32)
        lane = lax.broadcasted_iota(jnp.int32, q.shape, 1)
        qm = jnp.where((lane < HEAD_D) if m == 0 else (lane >= HEAD_D), q, 0.0).astype(BF16)
        return lax.dot_general(k_ref[0], qm, NT_DIMS, preferred_element_type=F32)

    def probs(s):
        mx = jnp.max(s, axis=0, keepdims=True)
        return jnp.exp2(s - mx).astype(BF16)

    def values(p):
        return functools.reduce(jnp.add, [jnp.dot(vt_ref[0, 0, j], p[j * ck:(j + 1) * ck], preferred_element_type=F32)
                                          for j in range(nchunk)])

    s_of, p_of, o_of = {}, {}, {}
    for t in range(len(chains) + 2):
        if t < len(chains):
            s_of[t] = scores(*chains[t])
        if 1 <= t <= len(chains):
            p_of[t - 1] = probs(s_of.pop(t - 1))
        if t >= 2:
            o_of[t - 2] = values(p_of.pop(t - 2))
    for c in range(tq // ck):
        o0, o1 = o_of[2 * c], o_of[2 * c + 1]
        ot = (o0[:HEAD_W] * (1.0 / o0[HEAD_W:HEAD_W + 1]) - o1[:HEAD_W] * (lam / o1[HEAD_W:HEAD_W + 1]))
        ms = jnp.mean(ot * ot, axis=0, keepdims=True)
        ot = (ot * lax.rsqrt(ms + EPS)) * gsub_ref[...] * (1.0 - lam_init)
        o_ref[0, c * ck:(c + 1) * ck, :] = ot.T.astype(BF16)


def diff_attention(q, k_all, vt_all, lam_vec, g_sub_col, lam_init, tq):
    b, n, _ = q.shape
    nkeys = k_all.shape[1]
    nchunk = nkeys // ATTN_CHUNK
    return pl.pallas_call(
        functools.partial(_attn_kernel, nchunk, lam_init),
        grid=(b, N_HEADS, n // tq),
        in_specs=[
            pl.BlockSpec((1, tq, HEAD_W), lambda bi, h, i: (bi, i, h)),
            pl.BlockSpec((4, HEAD_D), lambda bi, h, i: (0, 0)),
            pl.BlockSpec((HEAD_W, 1), lambda bi, h, i: (0, 0)),
            pl.BlockSpec((1, nkeys, HEAD_W), lambda bi, h, i: (bi, 0, h)),
            pl.BlockSpec((1, 1, nchunk, HEAD_W + ONES_ROWS, ATTN_CHUNK), lambda bi, h, i: (bi, h, 0, 0, 0)),
        ],
        out_specs=pl.BlockSpec((1, tq, HEAD_W), lambda bi, h, i: (bi, i, h)),
        out_shape=jax.ShapeDtypeStruct((b, n, ATTN_W), BF16),
        compiler_params=_params(("arbitrary", "arbitrary", "arbitrary")),
        name="diff_attention",
    )(q, lam_vec, g_sub_col, k_all, vt_all)


def _conv_kernel(n, rows, z_ref, w_ref, b_ref, lg_ref, lb_ref, o_ref, zp_ref, y_ref):
    zp_ref[0:CONV_PAD, :] = jnp.zeros((CONV_PAD, CONV_W), F32)
    zp_ref[CONV_PAD + n:2 * CONV_PAD + n, :] = jnp.zeros((CONV_PAD, CONV_W), F32)
    zp_ref[CONV_PAD:CONV_PAD + n, :] = z_ref[0]
    first = CONV_PAD - CONV_K // 2

    def chunk(i, carry):
        r0 = pl.multiple_of(i * rows, rows)
        for c in range(CONV_W // 128):
            lanes = slice(c * 128, (c + 1) * 128)
            total = rows + 2 * CONV_PAD
            win = zp_ref[pl.ds(r0, total), lanes]
            shifted = [win] + [pltpu.roll(win, total - s, 0) for s in range(1, 8)]
            acc = jnp.zeros((rows, 128), F32)
            for k in range(CONV_K):
                tiles, s = divmod(first + k, 8)
                acc = acc + shifted[s][8 * tiles:8 * tiles + rows, :] * w_ref[k:k + 1, lanes]
            y_ref[pl.ds(r0, rows), lanes] = acc + b_ref[:, lanes]
        return carry

    lax.fori_loop(0, n // rows, chunk, 0)
    y = y_ref[...]
    mu = jnp.mean(y, axis=-1, keepdims=True)
    d = y - mu
    var = jnp.mean(d * d, axis=-1, keepdims=True)
    yn = d * lax.rsqrt(var + EPS) * lg_ref[...] + lb_ref[...]
    o_ref[0] = (yn * jax.nn.sigmoid(yn)).astype(BF16)


def conformer_conv(z, w, bias, ln_g, ln_b):
    b, n, _ = z.shape
    vec = pl.BlockSpec((1, CONV_W), lambda bi: (0, 0))
    return pl.pallas_call(
        functools.partial(_conv_kernel, n, 64),
        grid=(b,),
        in_specs=[
            pl.BlockSpec((1, n, CONV_W), lambda bi: (bi, 0, 0)),
            pl.BlockSpec((CONV_K, CONV_W), lambda bi: (0, 0)),
            vec, vec, vec,
        ],
        out_specs=pl.BlockSpec((1, n, CONV_W), lambda bi: (bi, 0, 0)),
        out_shape=jax.ShapeDtypeStruct((b, n, CONV_W), BF16),
        scratch_shapes=[pltpu.VMEM((n + 2 * CONV_PAD, CONV_W), F32), pltpu.VMEM((n, CONV_W), F32)],
        compiler_params=_params(("arbitrary",)),
        name="conformer_conv",
    )(z, w, bias.reshape(1, CONV_W), ln_g.reshape(1, CONV_W), ln_b.reshape(1, CONV_W))


def _fourier_kernel(n, u_ref, wc_ref, cs_ref, o_ref, pq_ref):
    @pl.when(pl.program_id(1) == 0)
    def _():
        pq = jnp.dot(u_ref[0], wc_ref[...], preferred_element_type=F32)
        pq_ref[0:n, :] = pq[:, :FOUR_W].astype(BF16)
        pq_ref[n:2 * n, :] = pq[:, FOUR_W:].astype(BF16)

    o_ref[0] = jnp.dot(cs_ref[...], pq_ref[...], preferred_element_type=F32).astype(BF16)


def fourier_mix(u, wc, cs, tm=256):
    b, n, _ = u.shape
    tm = min(tm, n)
    return pl.pallas_call(
        functools.partial(_fourier_kernel, n),
        grid=(b, n // tm),
        in_specs=[
            pl.BlockSpec((1, n, FOUR_W), lambda bi, i: (bi, 0, 0)),
            pl.BlockSpec((FOUR_W, 2 * FOUR_W), lambda bi, i: (0, 0)),
            pl.BlockSpec((tm, 2 * n), lambda bi, i: (i, 0)),
        ],
        out_specs=pl.BlockSpec((1, tm, FOUR_W), lambda bi, i: (bi, i, 0)),
        out_shape=jax.ShapeDtypeStruct((b, n, FOUR_W), BF16),
        scratch_shapes=[pltpu.VMEM((2 * n, FOUR_W), BF16)],
        compiler_params=_params(("arbitrary", "arbitrary")),
        name="fourier_mix",
    )(u, wc, cs)


def dft_tables(n):
    j = jnp.arange(n, dtype=jnp.int32)
    ang = (2.0 * math.pi / n) * ((j[:, None] * j[None, :]) % n).astype(F32)
    cs = jnp.concatenate([jnp.cos(ang), -jnp.sin(ang)], axis=1) * (n ** -0.5)
    c = jnp.arange(FOUR_HD, dtype=jnp.int32)
    angc = (2.0 * math.pi / FOUR_HD) * ((c[:, None] * c[None, :]) % FOUR_HD).astype(F32)
    eye = jnp.eye(FOUR_HEADS, dtype=F32)
    wc = jnp.concatenate([jnp.kron(eye, jnp.cos(angc)), jnp.kron(eye, jnp.sin(angc))], axis=1) * (FOUR_HD ** -0.5)
    return cs.astype(BF16), wc.astype(BF16)


def _outproj_kernel(a_ref, c_ref, f_ref, w_ref, x_ref, gt_ref, g2_ref, sh_ref, sc_ref, wr_ref,
                    x1_ref, hp_ref, aff_ref):
    y = jnp.dot(a_ref[0], w_ref[0:ATTN_W, :], preferred_element_type=F32)
    y = y + jnp.dot(c_ref[0], w_ref[ATTN_W:ATTN_W + CONV_W, :], preferred_element_type=F32)
    y = y + jnp.dot(f_ref[0], w_ref[ATTN_W + CONV_W:, :], preferred_element_type=F32)
    x1 = x_ref[0] + gt_ref[0] * y
    x1_ref[0] = x1
    hb = _rms_mod(x1, g2_ref[...], sc_ref[0], sh_ref[0]).astype(BF16)
    bits = pltpu.bitcast(hb.astype(F32), jnp.uint32)
    hp_ref[0] = (bits[:, :HALF_D] >> 16) | bits[:, HALF_D:]
    logits = jnp.dot(hb, wr_ref[...], preferred_element_type=F32)
    lane = lax.broadcasted_iota(jnp.int32, logits.shape, 1)
    logits = jnp.where(lane < N_EXPERTS, logits, -1e30)
    ex = jnp.exp(logits - jnp.max(logits, axis=-1, keepdims=True))
    aff_ref[0] = ex / jnp.sum(ex, axis=-1, keepdims=True)


def outproj_router(attn, conv, four, w_out_bf, x, gt1, g2, sh2, sc2, wr_pad, tm=256):
    b, n, _ = x.shape
    row = lambda bi, i: (bi, i, 0)
    vec = lambda bi, i: (bi, 0, 0)
    return pl.pallas_call(
        _outproj_kernel,
        grid=(b, n // tm),
        in_specs=[
            pl.BlockSpec((1, tm, ATTN_W), row),
            pl.BlockSpec((1, tm, CONV_W), row),
            pl.BlockSpec((1, tm, FOUR_W), row),
            _resident((D_MODEL, D_MODEL), lambda bi, i: (0, 0)),
            pl.BlockSpec((1, tm, D_MODEL), row),
            pl.BlockSpec((1, 1, D_MODEL), vec),
            pl.BlockSpec((1, D_MODEL), lambda bi, i: (0, 0)),
            pl.BlockSpec((1, 1, D_MODEL), vec),
            pl.BlockSpec((1, 1, D_MODEL), vec),
            pl.BlockSpec((D_MODEL, ROUTER_LANES), lambda bi, i: (0, 0)),
        ],
        out_specs=[
            pl.BlockSpec((1, tm, D_MODEL), row),
            pl.BlockSpec((1, tm, HALF_D), row),
            pl.BlockSpec((1, tm, ROUTER_LANES), row),
        ],
        out_shape=[
            jax.ShapeDtypeStruct((b, n, D_MODEL), F32),
            jax.ShapeDtypeStruct((b, n, HALF_D), jnp.uint32),
            jax.ShapeDtypeStruct((b, n, ROUTER_LANES), F32),
        ],
        compiler_params=_params(("arbitrary", "arbitrary")),
        name="outproj_router",
    )(attn, conv, four, w_out_bf, x, gt1, g2, sh2, sc2, wr_pad)


def _prefix_count(x01, tri):
    carry = jnp.zeros((x01.shape[0], 1), F32)
    parts = []
    for c in range(x01.shape[1] // 128):
        loc = jnp.dot(x01[:, c * 128:(c + 1) * 128].astype(BF16), tri, preferred_element_type=F32)
        parts.append(loc + carry)
        carry = carry + loc[:, 127:128]
    return jnp.concatenate(parts, axis=1)


def _route_kernel(cap, aff_ref, idx_ref, gate_ref):
    aff = aff_ref[0]
    n = aff.shape[0]
    a = aff.T[0:N_EXPERTS, :]
    bits = pltpu.bitcast(a, jnp.int32)
    thr = jnp.zeros((N_EXPERTS, 1), jnp.int32)
    for bit in range(30, -1, -1):
        cand = thr | (1 << bit)
        cnt = jnp.sum(jnp.where(bits >= cand, 1.0, 0.0), axis=1, keepdims=True)
        thr = jnp.where(cnt >= cap, cand, thr)
    above = jnp.where(bits > thr, 1.0, 0.0)
    tied = jnp.where(bits == thr, 1.0, 0.0)
    r = lax.broadcasted_iota(jnp.int32, (128, 128), 0)
    c = lax.broadcasted_iota(jnp.int32, (128, 128), 1)
    tri = jnp.where(r <= c, 1.0, 0.0).astype(BF16)
    need = cap - jnp.sum(above, axis=1, keepdims=True)
    sel = above + tied * jnp.where(_prefix_count(tied, tri) <= need, 1.0, 0.0)
    slot_of = jnp.where(sel > 0.0, _prefix_count(sel, tri) - 1.0, -1.0)
    slot_t = jnp.concatenate([slot_of, jnp.full((ROUTER_LANES - N_EXPERTS, n), -1.0, F32)], axis=0).T
    slot = lax.broadcasted_iota(jnp.int32, (n, cap), 1).astype(F32)
    tok = lax.broadcasted_iota(jnp.int32, (n, cap), 0)
    for e in range(N_EXPERTS):
        hit = slot_t[:, e:e + 1] == slot
        idx_ref[0, e:e + 1, :] = jnp.sum(jnp.where(hit, tok, 0), axis=0, keepdims=True)
        gate_ref[0, e:e + 1, :] = jnp.sum(jnp.where(hit, aff[:, e:e + 1], 0.0), axis=0, keepdims=True)


def route(aff, cap):
    b, n, _ = aff.shape
    return pl.pallas_call(
        functools.partial(_route_kernel, cap),
        grid=(b,),
        in_specs=[pl.BlockSpec((1, n, ROUTER_LANES), lambda bi: (bi, 0, 0))],
        out_specs=[pl.BlockSpec((1, N_EXPERTS, cap), lambda bi: (bi, 0, 0))] * 2,
        out_shape=[jax.ShapeDtypeStruct((b, N_EXPERTS, cap), jnp.int32),
                   jax.ShapeDtypeStruct((b, N_EXPERTS, cap), F32)],
        compiler_params=_params(("arbitrary",)),
        name="route",
    )(aff)


def _ffn_kernel(group, cap, with_down, idx_ref, src_ref, wg_ref, wu_ref, *refs):
    wd_ref = refs[0] if with_down else None
    o_ref, xs_ref = refs[-2:]
    e = pl.program_id(0)
    bi = pl.program_id(1)
    base = (e * pl.num_programs(1) + bi) * (group * cap)
    for g in range(group):
        def gather_row(p, carry, g=g):
            tok = idx_ref[base + g * cap + p]
            xs_ref[pl.ds(g * cap + p, 1), :] = src_ref[g, pl.ds(tok, 1), :]
            return carry
        lax.fori_loop(0, cap, gather_row, 0, unroll=GATHER_UNROLL)
    w = xs_ref[...]
    lo = pltpu.bitcast(w << 16, F32).astype(BF16)
    hi = pltpu.bitcast(w & jnp.uint32(0xFFFF0000), F32).astype(BF16)

    def proj(w_ref):
        return (jnp.dot(lo, w_ref[0, :HALF_D, :], preferred_element_type=F32)
                + jnp.dot(hi, w_ref[0, HALF_D:, :], preferred_element_type=F32))

    gate = proj(wg_ref)
    hid = (gate * jax.nn.sigmoid(gate) * proj(wu_ref)).astype(BF16)
    res = jnp.dot(hid, wd_ref[0], preferred_element_type=F32) if with_down else hid
    for g in range(group):
        o_ref[g, 0] = res[g * cap:(g + 1) * cap, :]


def expert_ffn(hp, idx_flat, wg, wu, wd, cap, group):
    b, n, _ = hp.shape
    nb = b // group
    with_down = wd is not None
    out_w, out_dt = (D_MODEL, F32) if with_down else (EXPERT_FF, BF16)
    in_specs = [
        pl.BlockSpec((group, n, HALF_D), lambda e, bi, idx: (bi, 0, 0)),
        pl.BlockSpec((1, D_MODEL, EXPERT_FF), lambda e, bi, idx: (e, 0, 0)),
        pl.BlockSpec((1, D_MODEL, EXPERT_FF), lambda e, bi, idx: (e, 0, 0)),
    ]
    args = [idx_flat, hp, wg, wu]
    if with_down:
        in_specs.append(pl.BlockSpec((1, EXPERT_FF, D_MODEL), lambda e, bi, idx: (e, 0, 0)))
        args.append(wd)
    grid_spec = pltpu.PrefetchScalarGridSpec(
        num_scalar_prefetch=1,
        grid=(N_EXPERTS, nb),
        in_specs=in_specs,
        out_specs=pl.BlockSpec((group, 1, cap, out_w), lambda e, bi, idx: (bi, e, 0, 0)),
        scratch_shapes=[pltpu.VMEM((group * cap, HALF_D), jnp.uint32)],
    )
    return pl.pallas_call(
        functools.partial(_ffn_kernel, group, cap, with_down),
        grid_spec=grid_spec,
        out_shape=jax.ShapeDtypeStruct((b, N_EXPERTS, cap, out_w), out_dt),
        compiler_params=_params(("arbitrary", "arbitrary")),
        name="expert_ffn",
    )(*args)


def _down_scatter_kernel(cap, idx_ref, hid_ref, wd_ref, gate_ref, gt_ref, x1_hbm, o_ref, y_ref, sem):
    bi = pl.program_id(0)
    e = pl.program_id(1)

    @pl.when(e == 0)
    def _():
        load = pltpu.make_async_copy(x1_hbm.at[bi], o_ref.at[0], sem)
        load.start()
        load.wait()

    y = jnp.dot(hid_ref[0, 0], wd_ref[0], preferred_element_type=F32)
    y_ref[...] = y * gate_ref[0, 0] * gt_ref[0]
    base = (bi * N_EXPERTS + e) * cap

    def add_row(p, carry):
        tok = idx_ref[base + p]
        o_ref[0, pl.ds(tok, 1), :] = o_ref[0, pl.ds(tok, 1), :] + y_ref[pl.ds(p, 1), :]
        return carry

    lax.fori_loop(0, cap, add_row, 0, unroll=GATHER_UNROLL)


def expert_down_scatter(hid, idx_flat, gates, wd, x1, gt2):
    b, n, _ = x1.shape
    cap = hid.shape[2]
    grid_spec = pltpu.PrefetchScalarGridSpec(
        num_scalar_prefetch=1,
        grid=(b, N_EXPERTS),
        in_specs=[
            pl.BlockSpec((1, 1, cap, EXPERT_FF), lambda bi, e, idx: (bi, e, 0, 0)),
            pl.BlockSpec((1, EXPERT_FF, D_MODEL), lambda bi, e, idx: (e, 0, 0)),
            pl.BlockSpec((1, 1, cap, 1), lambda bi, e, idx: (bi, e, 0, 0)),
            pl.BlockSpec((1, 1, D_MODEL), lambda bi, e, idx: (bi, 0, 0)),
            pl.BlockSpec(memory_space=pl.ANY),
        ],
        out_specs=pl.BlockSpec((1, n, D_MODEL), lambda bi, e, idx: (bi, 0, 0)),
        scratch_shapes=[pltpu.VMEM((cap, D_MODEL), F32), pltpu.SemaphoreType.DMA(())],
    )
    return pl.pallas_call(
        functools.partial(_down_scatter_kernel, cap),
        grid_spec=grid_spec,
        out_shape=jax.ShapeDtypeStruct((b, n, D_MODEL), F32),
        compiler_params=_params(("arbitrary", "arbitrary")),
        name="expert_down_scatter",
    )(idx_flat, hid, wd, gates.reshape(b, N_EXPERTS, cap, 1), gt2, x1)


def _combine_kernel(ek, x_ref, gt_ref, ys_ref, idx_ref, gate_ref, o_ref, acc_ref):
    e = pl.program_id(2)

    @pl.when(e == 0)
    def _():
        acc_ref[...] = jnp.zeros(acc_ref.shape, F32)

    n = acc_ref.shape[0]
    cap = idx_ref.shape[-1]
    tok = lax.broadcasted_iota(jnp.int32, (n, cap), 0)
    for j in range(ek):
        onehot = jnp.where(tok == idx_ref[0, j], 1.0, 0.0).astype(BF16)
        yg = (ys_ref[0, j] * gate_ref[0, j]).astype(BF16)
        acc_ref[...] += jnp.dot(onehot, yg, preferred_element_type=F32)

    @pl.when(e == pl.num_programs(2) - 1)
    def _():
        o_ref[0] = x_ref[0] + gt_ref[0] * acc_ref[...]


def moe_combine(x1, gt2, ys, idx, gates, td, ek):
    b, n, _ = x1.shape
    cap = idx.shape[-1]
    return pl.pallas_call(
        functools.partial(_combine_kernel, ek),
        grid=(b, D_MODEL // td, N_EXPERTS // ek),
        in_specs=[
            pl.BlockSpec((1, n, td), lambda bi, d, e: (bi, 0, d)),
            pl.BlockSpec((1, 1, td), lambda bi, d, e: (bi, 0, d)),
            pl.BlockSpec((1, ek, cap, td), lambda bi, d, e: (bi, e, 0, d)),
            pl.BlockSpec((1, ek, 1, cap), lambda bi, d, e: (bi, e, 0, 0)),
            pl.BlockSpec((1, ek, cap, 1), lambda bi, d, e: (bi, e, 0, 0)),
        ],
        out_specs=pl.BlockSpec((1, n, td), lambda bi, d, e: (bi, 0, d)),
        out_shape=jax.ShapeDtypeStruct((b, n, D_MODEL), F32),
        scratch_shapes=[pltpu.VMEM((n, td), F32)],
        compiler_params=_params(("arbitrary", "arbitrary", "arbitrary")),
        name="moe_combine",
    )(x1, gt2, ys, idx.reshape(b, N_EXPERTS, 1, cap), gates.reshape(b, N_EXPERTS, cap, 1))


def _final_norm_kernel(x_ref, g_ref, o_ref):
    x = x_ref[0]
    ms = jnp.mean(x * x, axis=-1, keepdims=True)
    o_ref[0] = x * lax.rsqrt(ms + EPS) * g_ref[...]


def final_norm(x, g, tm=512):
    b, n, _ = x.shape
    tm = min(tm, n)
    row = pl.BlockSpec((1, tm, D_MODEL), lambda bi, i: (bi, i, 0))
    return pl.pallas_call(
        _final_norm_kernel,
        grid=(b, n // tm),
        in_specs=[row, pl.BlockSpec((1, D_MODEL), lambda bi, i: (0, 0))],
        out_specs=row,
        out_shape=jax.ShapeDtypeStruct(x.shape, F32),
        compiler_params=_params(("arbitrary", "arbitrary")),
        name="final_norm",
    )(x, g.reshape(1, D_MODEL))


def rope_tables(n):
    pos = jnp.arange(n, dtype=jnp.int32)
    rc = jnp.stack([pos // GRID_W, pos % GRID_W], axis=-1).astype(F32)
    inv_freq = ROPE_THETA ** (-jnp.arange(ROPE_FREQS, dtype=F32) / ROPE_FREQS)
    ang = rc[:, :, None, None] * inv_freq
    ang = jnp.broadcast_to(ang, (n, 2, 2, ROPE_FREQS)).reshape(n, HEAD_D)
    ang = jnp.concatenate([ang, ang], axis=-1)
    first_half = (jnp.arange(HEAD_W) % (2 * ROPE_FREQS)) < ROPE_FREQS
    sin = jnp.sin(ang)
    return jnp.cos(ang), jnp.where(first_half, -sin, 0.0), jnp.where(first_half, 0.0, sin)


def moe_latent(x1, gt2, hp, aff, wg, wu, wd):
    b, n, _ = hp.shape
    cap = (CAPACITY_FACTOR * n) // N_EXPERTS
    idx, gates = route(aff, cap)
    hid = expert_ffn(hp, jnp.swapaxes(idx, 0, 1).reshape(-1), wg, wu, None, cap, group=1)
    return expert_down_scatter(hid, idx.reshape(-1), gates, wd, x1, gt2)


def moe_context(x1, gt2, hp, aff, wg, wu, wd, group):
    b, n, _ = hp.shape
    cap = (CAPACITY_FACTOR * n) // N_EXPERTS
    idx, gates = route(aff, cap)
    ys = expert_ffn(hp, jnp.swapaxes(idx, 0, 1).reshape(-1), wg, wu, wd, cap, group)
    return moe_combine(x1, gt2, ys, idx, gates, td=1024, ek=N_EXPERTS)


def kernel(x, c, ctx, c_ctx, w_ada, b_ada, g_norm1, w_in, diff_lambda, g_sub, conv_w, conv_b, conv_ln_g,
           conv_ln_b, w_out, g_norm2, w_router, w_gate, w_up, w_down, g_final):
    b, n, d = x.shape
    nc = ctx.shape[1]
    pad_rows = (-(b + 1)) % 8
    cc = jnp.concatenate([c, c_ctx[None, :], jnp.zeros((pad_rows, d), F32)], axis=0)
    mod_all = ada_modulation(cc, w_ada, b_ada)

    cos, sa, sb = rope_tables(n)
    cos_c = jnp.ones((nc, HEAD_W), F32)
    zero_c = jnp.zeros((nc, HEAD_W), F32)
    cs_n, wc = dft_tables(n)
    cs_c, _ = dft_tables(nc)

    xc = ctx
    for l in range(DEPTH):
        last = l == DEPTH - 1
        lam_init = 0.8 - 0.6 * math.exp(-0.3 * l)
        mod = mod_all[l, :b].reshape(b, 1, 6 * d)
        modc = jnp.broadcast_to(mod_all[l, b].reshape(1, 1, 6 * d), (b, 1, 6 * d))
        sh1, sc1, gt1, sh2, sc2, gt2 = [mod[..., i * d:(i + 1) * d] for i in range(6)]
        csh1, csc1, cgt1, csh2, csc2, cgt2 = [modc[..., i * d:(i + 1) * d] for i in range(6)]
        g1 = g_norm1[l].reshape(1, d)
        g2 = g_norm2[l].reshape(1, d)
        w_in_bf = w_in[l].astype(BF16)
        w_out_bf = w_out[l].astype(BF16)
        wr_pad = jnp.pad(w_router[l], ((0, 0), (0, ROUTER_LANES - N_EXPERTS))).astype(BF16)
        wg, wu, wd = w_gate[l].astype(BF16), w_up[l].astype(BF16), w_down[l].astype(BF16)
        wvt_bf = w_in[l][:, V0:G0].T.astype(BF16)
        gs = g_sub[l].reshape(HEAD_W, 1)
        conv_args = (conv_w[l], conv_b[l], conv_ln_g[l], conv_ln_b[l])

        qc, kc, vct, zc, ufc = inproj(xc, csh1, csc1, g1, w_in_bf, wvt_bf, cos_c, zero_c, zero_c)
        q, k_all, vt_all, z, uf = inproj(x, sh1, sc1, g1, w_in_bf, wvt_bf, cos, sa, sb, kv_prefix=(kc, vct))

        attn = diff_attention(q, k_all, vt_all, diff_lambda[l], gs, lam_init, tq=2 * ATTN_CHUNK)
        conv = conformer_conv(z, *conv_args)
        four = fourier_mix(uf, wc, cs_n)
        x1, hp, aff = outproj_router(attn, conv, four, w_out_bf, x, gt1, g2, sh2, sc2, wr_pad)
        x = moe_latent(x1, gt2, hp, aff, wg, wu, wd)

        if not last:
            attn_c = diff_attention(qc, kc, vct, diff_lambda[l], gs, lam_init, tq=ATTN_CHUNK)
            conv_c = conformer_conv(zc, *conv_args)
            four_c = fourier_mix(ufc, wc, cs_c)
            xc1, hpc, affc = outproj_router(attn_c, conv_c, four_c, w_out_bf, xc, cgt1, g2, csh2, csc2, wr_pad)
            xc = moe_context(xc1, cgt2, hpc, affc, wg, wu, wd, group=8)
    return final_norm(x, g_final)
```

```python
import functools
import math

import jax
import jax.numpy as jnp
from jax import lax
from jax.experimental import pallas as pl
from jax.experimental.pallas import tpu as pltpu

D_MODEL = 2048
DEPTH = 4
GRID_W = 64
N_HEADS = 8
HEAD_D = 64
HEAD_W = 2 * HEAD_D
ATTN_W = N_HEADS * HEAD_W
QK_W = ATTN_W
ROPE_THETA = 10000.0
ROPE_FREQS = HEAD_D // 4
CONV_W = 512
CONV_K = 31
CONV_PAD = 16
FOUR_HEADS = 4
FOUR_HD = 128
FOUR_W = FOUR_HEADS * FOUR_HD
K0 = QK_W
V0 = 2 * QK_W
G0 = V0 + ATTN_W
F0 = G0 + 2 * CONV_W
IN_W = F0 + FOUR_W
N_EXPERTS = 16
EXPERT_FF = 1024
CAPACITY_FACTOR = 2
EPS = 1e-6
ROUTER_LANES = 128
HALF_D = D_MODEL // 2
GATHER_UNROLL = 8
ROUTE_GROUP = 4
ONES_ROWS = 16
ATTN_CHUNK = 256

VMEM_LIMIT = 56 * 1024 * 1024

F32 = jnp.float32
BF16 = jnp.bfloat16
NT_DIMS = (((1,), (1,)), ((), ()))


def _params(sem, vmem=VMEM_LIMIT):
    return pltpu.CompilerParams(dimension_semantics=sem, vmem_limit_bytes=vmem)


def _resident(shape, index_map):
    return pl.BlockSpec(shape, index_map, pipeline_mode=pl.Buffered(1))


def _ada_kernel(c_ref, w_ref, b_ref, o_ref):
    c = c_ref[...]
    a = (c * jax.nn.sigmoid(c)).astype(BF16)
    w = w_ref[0].astype(BF16)
    o_ref[0] = jnp.dot(a, w, preferred_element_type=F32) + b_ref[0]


def ada_modulation(cc, w_ada, b_ada):
    rows = cc.shape[0]
    tn = 1024
    return pl.pallas_call(
        _ada_kernel,
        grid=(DEPTH, 6 * D_MODEL // tn),
        in_specs=[
            pl.BlockSpec((rows, D_MODEL), lambda l, j: (0, 0)),
            pl.BlockSpec((1, D_MODEL, tn), lambda l, j: (l, 0, j)),
            pl.BlockSpec((1, 1, tn), lambda l, j: (l, 0, j)),
        ],
        out_specs=pl.BlockSpec((1, rows, tn), lambda l, j: (l, 0, j)),
        out_shape=jax.ShapeDtypeStruct((DEPTH, rows, 6 * D_MODEL), F32),
        compiler_params=_params(("arbitrary", "arbitrary")),
        name="ada_modulation",
    )(cc, w_ada, b_ada.reshape(DEPTH, 1, 6 * D_MODEL))


def _rms_mod(x, g, sc, sh):
    ms = jnp.mean(x * x, axis=-1, keepdims=True)
    return (x * lax.rsqrt(ms + EPS)) * g * (1.0 + sc) + sh


def _inproj_kernel(nprefix, *refs):
    if nprefix == 0:
        _inproj_body(*refs)
        return
    ins, (kc_ref, vtc_ref), outs = refs[:9], refs[9:11], refs[11:]
    i = pl.program_id(1)

    @pl.when(i < nprefix)
    def _():
        outs[1][...] = kc_ref[...]
        outs[2][...] = vtc_ref[...]

    @pl.when(i >= nprefix)
    def _():
        _inproj_body(*ins, *outs)


def _inproj_body(x_ref, sh_ref, sc_ref, g_ref, w_ref, wvt_ref, cos_ref, sa_ref, sb_ref,
                 q_ref, k_ref, vt_ref, z_ref, f_ref):
    hb = _rms_mod(x_ref[0], g_ref[...], sc_ref[0], sh_ref[0]).astype(BF16)
    cos = cos_ref[...]
    sa = sa_ref[...]
    sb = sb_ref[...]

    def rope_store(col0, out_ref, scale):
        t = jnp.dot(hb, w_ref[:, col0:col0 + QK_W], preferred_element_type=F32)
        for h in range(N_HEADS):
            th = t[:, h * HEAD_W:(h + 1) * HEAD_W]
            r = th * cos + pltpu.roll(th, HEAD_W - ROPE_FREQS, 1) * sa + pltpu.roll(th, ROPE_FREQS, 1) * sb
            out_ref[0, :, h * HEAD_W:(h + 1) * HEAD_W] = (r * scale).astype(BF16)

    rope_store(0, q_ref, HEAD_D ** -0.5 * math.log2(math.e))
    rope_store(K0, k_ref, 1.0)
    vt = lax.dot_general(wvt_ref[...], hb, NT_DIMS, preferred_element_type=F32).astype(BF16)
    for h in range(N_HEADS):
        vt_ref[0, h, 0, 0:HEAD_W, :] = vt[h * HEAD_W:(h + 1) * HEAD_W, :]
        vt_ref[0, h, 0, HEAD_W:HEAD_W + ONES_ROWS, :] = jnp.ones((ONES_ROWS, ATTN_CHUNK), BF16)
    a = jnp.dot(hb, w_ref[:, G0:G0 + CONV_W], preferred_element_type=F32)
    g = jnp.dot(hb, w_ref[:, G0 + CONV_W:F0], preferred_element_type=F32)
    z_ref[0] = a * jax.nn.sigmoid(g)
    f_ref[0] = jnp.dot(hb, w_ref[:, F0:IN_W], preferred_element_type=F32).astype(BF16)


def inproj(x, sh, sc, g, w_in_bf, wvt_bf, cos, sa, sb, kv_prefix=None):
    b, n, _ = x.shape
    tm = ATTN_CHUNK
    nprefix = 0 if kv_prefix is None else kv_prefix[0].shape[1] // ATTN_CHUNK
    nchunk = nprefix + n // ATTN_CHUNK
    step = lambda i: jnp.maximum(i - nprefix, 0)
    row = lambda bi, i: (bi, step(i), 0)
    vec = lambda bi, i: (bi, 0, 0)
    tab = lambda bi, i: (step(i), 0)
    out = lambda w, dt: jax.ShapeDtypeStruct((b, n, w), dt)
    in_specs = [
        pl.BlockSpec((1, tm, D_MODEL), row),
        pl.BlockSpec((1, 1, D_MODEL), vec),
        pl.BlockSpec((1, 1, D_MODEL), vec),
        pl.BlockSpec((1, D_MODEL), lambda bi, i: (0, 0)),
        _resident((D_MODEL, IN_W), lambda bi, i: (0, 0)),
        _resident((ATTN_W, D_MODEL), lambda bi, i: (0, 0)),
        pl.BlockSpec((tm, HEAD_W), tab),
        pl.BlockSpec((tm, HEAD_W), tab),
        pl.BlockSpec((tm, HEAD_W), tab),
    ]
    args = [x, sh, sc, g, w_in_bf, wvt_bf, cos, sa, sb]
    if nprefix:
        pre = lambda i: jnp.minimum(i, nprefix - 1)
        in_specs += [pl.BlockSpec((1, tm, QK_W), lambda bi, i: (bi, pre(i), 0)),
                     pl.BlockSpec((1, N_HEADS, 1, HEAD_W + ONES_ROWS, tm), lambda bi, i: (bi, 0, pre(i), 0, 0))]
        args += list(kv_prefix)
    return pl.pallas_call(
        functools.partial(_inproj_kernel, nprefix),
        grid=(b, nchunk),
        in_specs=in_specs,
        out_specs=[
            pl.BlockSpec((1, tm, QK_W), row),
            pl.BlockSpec((1, tm, QK_W), lambda bi, i: (bi, i, 0)),
            pl.BlockSpec((1, N_HEADS, 1, HEAD_W + ONES_ROWS, tm), lambda bi, i: (bi, 0, i, 0, 0)),
            pl.BlockSpec((1, tm, CONV_W), row),
            pl.BlockSpec((1, tm, FOUR_W), row),
        ],
        out_shape=[out(QK_W, BF16),
                   jax.ShapeDtypeStruct((b, nchunk * ATTN_CHUNK, QK_W), BF16),
                   jax.ShapeDtypeStruct((b, N_HEADS, nchunk, HEAD_W + ONES_ROWS, ATTN_CHUNK), BF16),
                   out(CONV_W, F32), out(FOUR_W, BF16)],
        compiler_params=_params(("arbitrary", "arbitrary")),
        name="inproj",
    )(*args)


def _attn_kernel(nchunk, lam_init, q_ref, lam_ref, gsub_ref, k_ref, vt_ref, o_ref):
    tq = q_ref.shape[1]
    ck = ATTN_CHUNK
    lv = lam_ref[...]
    lam = (jnp.exp(jnp.sum(lv[0:1] * lv[1:2], axis=-1, keepdims=True))
           - jnp.exp(jnp.sum(lv[2:3] * lv[3:4], axis=-1, keepdims=True)) + lam_init)
    chains = [(c, m) for c in range(tq // ck) for m in range(2)]

    def scores(c, m):
        q = q_ref[0, c * ck:(c + 1) * ck, :].astype(F32)
        lane = lax.broadcasted_iota(jnp.int32, q.shape, 1)
        qm = jnp.where((lane < HEAD_D) if m == 0 else (lane >= HEAD_D), q, 0.0).astype(BF16)
        return lax.dot_general(k_ref[0], qm, NT_DIMS, preferred_element_type=F32)

    def probs(s):
        mx = jnp.max(s, axis=0, keepdims=True)
        return jnp.exp2(s - mx).astype(BF16)

    def values(p):
        return functools.reduce(jnp.add, [jnp.dot(vt_ref[0, 0, j], p[j * ck:(j + 1) * ck], preferred_element_type=F32)
                                          for j in range(nchunk)])

    s_of, p_of, o_of = {}, {}, {}
    for t in range(len(chains) + 2):
        if t < len(chains):
            s_of[t] = scores(*chains[t])
        if 1 <= t <= len(chains):
            p_of[t - 1] = probs(s_of.pop(t - 1))
        if t >= 2:
            o_of[t - 2] = values(p_of.pop(t - 2))
    for c in range(tq // ck):
        o0, o1 = o_of[2 * c], o_of[2 * c + 1]
        ot = (o0[:HEAD_W] * (1.0 / o0[HEAD_W:HEAD_W + 1]) - o1[:HEAD_W] * (lam / o1[HEAD_W:HEAD_W + 1]))
        ms = jnp.mean(ot * ot, axis=0, keepdims=True)
        ot = (ot * lax.rsqrt(ms + EPS)) * gsub_ref[...] * (1.0 - lam_init)
        o_ref[0, c * ck:(c + 1) * ck, :] = ot.T.astype(BF16)


def diff_attention(q, k_all, vt_all, lam_vec, g_sub_col, lam_init, tq):
    b, n, _ = q.shape
    nkeys = k_all.shape[1]
    nchunk = nkeys // ATTN_CHUNK
    return pl.pallas_call(
        functools.partial(_attn_kernel, nchunk, lam_init),
        grid=(b, N_HEADS, n // tq),
        in_specs=[
            pl.BlockSpec((1, tq, HEAD_W), lambda bi, h, i: (bi, i, h)),
            pl.BlockSpec((4, HEAD_D), lambda bi, h, i: (0, 0)),
            pl.BlockSpec((HEAD_W, 1), lambda bi, h, i: (0, 0)),
            pl.BlockSpec((1, nkeys, HEAD_W), lambda bi, h, i: (bi, 0, h)),
            pl.BlockSpec((1, 1, nchunk, HEAD_W + ONES_ROWS, ATTN_CHUNK), lambda bi, h, i: (bi, h, 0, 0, 0)),
        ],
        out_specs=pl.BlockSpec((1, tq, HEAD_W), lambda bi, h, i: (bi, i, h)),
        out_shape=jax.ShapeDtypeStruct((b, n, ATTN_W), BF16),
        compiler_params=_params(("arbitrary", "arbitrary", "arbitrary")),
        name="diff_attention",
    )(q, lam_vec, g_sub_col, k_all, vt_all)


def _conv_kernel(n, rows, z_ref, w_ref, b_ref, lg_ref, lb_ref, o_ref, zp_ref, y_ref):
    zp_ref[0:CONV_PAD, :] = jnp.zeros((CONV_PAD, CONV_W), F32)
    zp_ref[CONV_PAD + n:2 * CONV_PAD + n, :] = jnp.zeros((CONV_PAD, CONV_W), F32)
    zp_ref[CONV_PAD:CONV_PAD + n, :] = z_ref[0]
    first = CONV_PAD - CONV_K // 2

    def chunk(i, carry):
        r0 = pl.multiple_of(i * rows, rows)
        for c in range(CONV_W // 128):
            lanes = slice(c * 128, (c + 1) * 128)
            total = rows + 2 * CONV_PAD
            win = zp_ref[pl.ds(r0, total), lanes]
            shifted = [win] + [pltpu.roll(win, total - s, 0) for s in range(1, 8)]
            acc = jnp.zeros((rows, 128), F32)
            for k in range(CONV_K):
                tiles, s = divmod(first + k, 8)
                acc = acc + shifted[s][8 * tiles:8 * tiles + rows, :] * w_ref[k:k + 1, lanes]
            y_ref[pl.ds(r0, rows), lanes] = acc + b_ref[:, lanes]
        return carry

    lax.fori_loop(0, n // rows, chunk, 0)
    y = y_ref[...]
    mu = jnp.mean(y, axis=-1, keepdims=True)
    d = y - mu
    var = jnp.mean(d * d, axis=-1, keepdims=True)
    yn = d * lax.rsqrt(var + EPS) * lg_ref[...] + lb_ref[...]
    o_ref[0] = (yn * jax.nn.sigmoid(yn)).astype(BF16)


def conformer_conv(z, w, bias, ln_g, ln_b):
    b, n, _ = z.shape
    vec = pl.BlockSpec((1, CONV_W), lambda bi: (0, 0))
    return pl.pallas_call(
        functools.partial(_conv_kernel, n, 64),
        grid=(b,),
        in_specs=[
            pl.BlockSpec((1, n, CONV_W), lambda bi: (bi, 0, 0)),
            pl.BlockSpec((CONV_K, CONV_W), lambda bi: (0, 0)),
            vec, vec, vec,
        ],
        out_specs=pl.BlockSpec((1, n, CONV_W), lambda bi: (bi, 0, 0)),
        out_shape=jax.ShapeDtypeStruct((b, n, CONV_W), BF16),
        scratch_shapes=[pltpu.VMEM((n + 2 * CONV_PAD, CONV_W), F32), pltpu.VMEM((n, CONV_W), F32)],
        compiler_params=_params(("arbitrary",)),
        name="conformer_conv",
    )(z, w, bias.reshape(1, CONV_W), ln_g.reshape(1, CONV_W), ln_b.reshape(1, CONV_W))


def _fourier_kernel(n, u_ref, wc_ref, cs_ref, o_ref, pq_ref):
    @pl.when(pl.program_id(1) == 0)
    def _():
        pq = jnp.dot(u_ref[0], wc_ref[...], preferred_element_type=F32)
        pq_ref[0:n, :] = pq[:, :FOUR_W].astype(BF16)
        pq_ref[n:2 * n, :] = pq[:, FOUR_W:].astype(BF16)

    o_ref[0] = jnp.dot(cs_ref[...], pq_ref[...], preferred_element_type=F32).astype(BF16)


def fourier_mix(u, wc, cs, tm=256):
    b, n, _ = u.shape
    tm = min(tm, n)
    return pl.pallas_call(
        functools.partial(_fourier_kernel, n),
        grid=(b, n // tm),
        in_specs=[
            pl.BlockSpec((1, n, FOUR_W), lambda bi, i: (bi, 0, 0)),
            pl.BlockSpec((FOUR_W, 2 * FOUR_W), lambda bi, i: (0, 0)),
            pl.BlockSpec((tm, 2 * n), lambda bi, i: (i, 0)),
        ],
        out_specs=pl.BlockSpec((1, tm, FOUR_W), lambda bi, i: (bi, i, 0)),
        out_shape=jax.ShapeDtypeStruct((b, n, FOUR_W), BF16),
        scratch_shapes=[pltpu.VMEM((2 * n, FOUR_W), BF16)],
        compiler_params=_params(("arbitrary", "arbitrary")),
        name="fourier_mix",
    )(u, wc, cs)


def dft_tables(n):
    j = jnp.arange(n, dtype=jnp.int32)
    ang = (2.0 * math.pi / n) * ((j[:, None] * j[None, :]) % n).astype(F32)
    cs = jnp.concatenate([jnp.cos(ang), -jnp.sin(ang)], axis=1) * (n ** -0.5)
    c = jnp.arange(FOUR_HD, dtype=jnp.int32)
    angc = (2.0 * math.pi / FOUR_HD) * ((c[:, None] * c[None, :]) % FOUR_HD).astype(F32)
    eye = jnp.eye(FOUR_HEADS, dtype=F32)
    wc = jnp.concatenate([jnp.kron(eye, jnp.cos(angc)), jnp.kron(eye, jnp.sin(angc))], axis=1) * (FOUR_HD ** -0.5)
    return cs.astype(BF16), wc.astype(BF16)


def _outproj_kernel(a_ref, c_ref, f_ref, w_ref, x_ref, gt_ref, g2_ref, sh_ref, sc_ref, wr_ref,
                    x1_ref, hp_ref, aff_ref):
    y = jnp.dot(a_ref[0], w_ref[0:ATTN_W, :], preferred_element_type=F32)
    y = y + jnp.dot(c_ref[0], w_ref[ATTN_W:ATTN_W + CONV_W, :], preferred_element_type=F32)
    y = y + jnp.dot(f_ref[0], w_ref[ATTN_W + CONV_W:, :], preferred_element_type=F32)
    x1 = x_ref[0] + gt_ref[0] * y
    x1_ref[0] = x1
    hb = _rms_mod(x1, g2_ref[...], sc_ref[0], sh_ref[0]).astype(BF16)
    bits = pltpu.bitcast(hb.astype(F32), jnp.uint32)
    hp_ref[0] = (bits[:, :HALF_D] >> 16) | bits[:, HALF_D:]
    logits = jnp.dot(hb, wr_ref[...], preferred_element_type=F32)
    lane = lax.broadcasted_iota(jnp.int32, logits.shape, 1)
    logits = jnp.where(lane < N_EXPERTS, logits, -1e30)
    ex = jnp.exp(logits - jnp.max(logits, axis=-1, keepdims=True))
    aff_ref[0] = ex / jnp.sum(ex, axis=-1, keepdims=True)


def outproj_router(attn, conv, four, w_out_bf, x, gt1, g2, sh2, sc2, wr_pad, tm=256):
    b, n, _ = x.shape
    row = lambda bi, i: (bi, i, 0)
    vec = lambda bi, i: (bi, 0, 0)
    return pl.pallas_call(
        _outproj_kernel,
        grid=(b, n // tm),
        in_specs=[
            pl.BlockSpec((1, tm, ATTN_W), row),
            pl.BlockSpec((1, tm, CONV_W), row),
            pl.BlockSpec((1, tm, FOUR_W), row),
            _resident((D_MODEL, D_MODEL), lambda bi, i: (0, 0)),
            pl.BlockSpec((1, tm, D_MODEL), row),
            pl.BlockSpec((1, 1, D_MODEL), vec),
            pl.BlockSpec((1, D_MODEL), lambda bi, i: (0, 0)),
            pl.BlockSpec((1, 1, D_MODEL), vec),
            pl.BlockSpec((1, 1, D_MODEL), vec),
            pl.BlockSpec((D_MODEL, ROUTER_LANES), lambda bi, i: (0, 0)),
        ],
        out_specs=[
            pl.BlockSpec((1, tm, D_MODEL), row),
            pl.BlockSpec((1, tm, HALF_D), row),
            pl.BlockSpec((1, tm, ROUTER_LANES), row),
        ],
        out_shape=[
            jax.ShapeDtypeStruct((b, n, D_MODEL), F32),
            jax.ShapeDtypeStruct((b, n, HALF_D), jnp.uint32),
            jax.ShapeDtypeStruct((b, n, ROUTER_LANES), F32),
        ],
        compiler_params=_params(("arbitrary", "arbitrary")),
        name="outproj_router",
    )(attn, conv, four, w_out_bf, x, gt1, g2, sh2, sc2, wr_pad)


def _prefix_count(x01, tri):
    carry = jnp.zeros((x01.shape[0], 1), F32)
    parts = []
    for c in range(x01.shape[1] // 128):
        loc = jnp.dot(x01[:, c * 128:(c + 1) * 128].astype(BF16), tri, preferred_element_type=F32)
        parts.append(loc + carry)
        carry = carry + loc[:, 127:128]
    return jnp.concatenate(parts, axis=1)


def _route_kernel(cap, aff_ref, idx_ref, gate_ref):
    group, n, _ = aff_ref.shape
    rows = group * N_EXPERTS
    a = jnp.concatenate([aff_ref[g].T[0:N_EXPERTS, :] for g in range(group)], axis=0)
    bits = pltpu.bitcast(a, jnp.int32)
    thr = jnp.zeros((rows, 1), jnp.int32)
    for bit in range(30, -1, -1):
        cand = thr | (1 << bit)
        cnt = jnp.sum(jnp.where(bits >= cand, 1.0, 0.0), axis=1, keepdims=True)
        thr = jnp.where(cnt >= cap, cand, thr)
    above = jnp.where(bits > thr, 1.0, 0.0)
    tied = jnp.where(bits == thr, 1.0, 0.0)
    r = lax.broadcasted_iota(jnp.int32, (128, 128), 0)
    c = lax.broadcasted_iota(jnp.int32, (128, 128), 1)
    tri = jnp.where(r <= c, 1.0, 0.0).astype(BF16)
    need = cap - jnp.sum(above, axis=1, keepdims=True)
    sel = above + tied * jnp.where(_prefix_count(tied, tri) <= need, 1.0, 0.0)
    slot_of = jnp.where(sel > 0.0, _prefix_count(sel, tri) - 1.0, -1.0)
    pad = jnp.full((ROUTER_LANES - rows, n), -1.0, F32)
    slot_t = jnp.concatenate([slot_of, pad], axis=0).T if rows < ROUTER_LANES else slot_of.T
    slot = lax.broadcasted_iota(jnp.int32, (n, cap), 1).astype(F32)
    tok = lax.broadcasted_iota(jnp.int32, (n, cap), 0)
    for g in range(group):
        aff = aff_ref[g]
        for e in range(N_EXPERTS):
            col = g * N_EXPERTS + e
            hit = slot_t[:, col:col + 1] == slot
            idx_ref[g, e:e + 1, :] = jnp.sum(jnp.where(hit, tok, 0), axis=0, keepdims=True)
            gate_ref[g, e:e + 1, :] = jnp.sum(jnp.where(hit, aff[:, e:e + 1], 0.0), axis=0, keepdims=True)


def route(aff, cap):
    b, n, _ = aff.shape
    return pl.pallas_call(
        functools.partial(_route_kernel, cap),
        grid=(b // ROUTE_GROUP,),
        in_specs=[pl.BlockSpec((ROUTE_GROUP, n, ROUTER_LANES), lambda bi: (bi, 0, 0))],
        out_specs=[pl.BlockSpec((ROUTE_GROUP, N_EXPERTS, cap), lambda bi: (bi, 0, 0))] * 2,
        out_shape=[jax.ShapeDtypeStruct((b, N_EXPERTS, cap), jnp.int32),
                   jax.ShapeDtypeStruct((b, N_EXPERTS, cap), F32)],
        compiler_params=_params(("arbitrary",)),
        name="route",
    )(aff)


def _ffn_kernel(group, cap, with_down, idx_ref, src_ref, wg_ref, wu_ref, *refs):
    wd_ref = refs[0] if with_down else None
    o_ref, xs_ref = refs[-2:]
    e = pl.program_id(0)
    bi = pl.program_id(1)
    base = (e * pl.num_programs(1) + bi) * (group * cap)
    for g in range(group):
        def gather_row(p, carry, g=g):
            tok = idx_ref[base + g * cap + p]
            xs_ref[pl.ds(g * cap + p, 1), :] = src_ref[g, pl.ds(tok, 1), :]
            return carry
        lax.fori_loop(0, cap, gather_row, 0, unroll=GATHER_UNROLL)
    w = xs_ref[...]
    lo = pltpu.bitcast(w << 16, F32).astype(BF16)
    hi = pltpu.bitcast(w & jnp.uint32(0xFFFF0000), F32).astype(BF16)

    def proj(w_ref):
        return (jnp.dot(lo, w_ref[0, :HALF_D, :], preferred_element_type=F32)
                + jnp.dot(hi, w_ref[0, HALF_D:, :], preferred_element_type=F32))

    gate = proj(wg_ref)
    hid = (gate * jax.nn.sigmoid(gate) * proj(wu_ref)).astype(BF16)
    res = jnp.dot(hid, wd_ref[0], preferred_element_type=F32) if with_down else hid
    for g in range(group):
        o_ref[g, 0] = res[g * cap:(g + 1) * cap, :]


def expert_ffn(hp, idx_flat, wg, wu, wd, cap, group):
    b, n, _ = hp.shape
    nb = b // group
    with_down = wd is not None
    out_w, out_dt = (D_MODEL, F32) if with_down else (EXPERT_FF, BF16)
    in_specs = [
        pl.BlockSpec((group, n, HALF_D), lambda e, bi, idx: (bi, 0, 0)),
        pl.BlockSpec((1, D_MODEL, EXPERT_FF), lambda e, bi, idx: (e, 0, 0)),
        pl.BlockSpec((1, D_MODEL, EXPERT_FF), lambda e, bi, idx: (e, 0, 0)),
    ]
    args = [idx_flat, hp, wg, wu]
    if with_down:
        in_specs.append(pl.BlockSpec((1, EXPERT_FF, D_MODEL), lambda e, bi, idx: (e, 0, 0)))
        args.append(wd)
    grid_spec = pltpu.PrefetchScalarGridSpec(
        num_scalar_prefetch=1,
        grid=(N_EXPERTS, nb),
        in_specs=in_specs,
        out_specs=pl.BlockSpec((group, 1, cap, out_w), lambda e, bi, idx: (bi, e, 0, 0)),
        scratch_shapes=[pltpu.VMEM((group * cap, HALF_D), jnp.uint32)],
    )
    return pl.pallas_call(
        functools.partial(_ffn_kernel, group, cap, with_down),
        grid_spec=grid_spec,
        out_shape=jax.ShapeDtypeStruct((b, N_EXPERTS, cap, out_w), out_dt),
        compiler_params=_params(("arbitrary", "arbitrary")),
        name="expert_ffn",
    )(*args)


def _down_scatter_kernel(cap, idx_ref, hid_ref, wd_ref, gate_ref, gt_ref, x1_hbm, o_ref, ya_ref, yb_ref, sem):
    bi = pl.program_id(0)
    e = pl.program_id(1)

    @pl.when(e == 0)
    def _():
        load = pltpu.make_async_copy(x1_hbm.at[bi], o_ref.at[0], sem)
        load.start()
        load.wait()
        yb_ref[...] = jnp.zeros(yb_ref.shape, F32)

    base = (bi * N_EXPERTS + e) * cap
    prev = jnp.maximum(base - cap, 0)

    def add_row(y_ref, at, p):
        tok = idx_ref[at + p]
        o_ref[0, pl.ds(tok, 1), :] = o_ref[0, pl.ds(tok, 1), :] + y_ref[pl.ds(p, 1), :]

    def step(y_new, y_old):
        for p in range(cap):
            add_row(y_old, prev, p)
        y = jnp.dot(hid_ref[0, 0], wd_ref[0], preferred_element_type=F32)
        y_new[...] = y * gate_ref[0, 0] * gt_ref[0]

        @pl.when(e == pl.num_programs(1) - 1)
        def _():
            def body(p, carry):
                add_row(y_new, base, p)
                return carry
            lax.fori_loop(0, cap, body, 0, unroll=GATHER_UNROLL)

    @pl.when(e % 2 == 0)
    def _():
        step(ya_ref, yb_ref)

    @pl.when(e % 2 == 1)
    def _():
        step(yb_ref, ya_ref)


def expert_down_scatter(hid, idx_flat, gates, wd, x1, gt2):
    b, n, _ = x1.shape
    cap = hid.shape[2]
    grid_spec = pltpu.PrefetchScalarGridSpec(
        num_scalar_prefetch=1,
        grid=(b, N_EXPERTS),
        in_specs=[
            pl.BlockSpec((1, 1, cap, EXPERT_FF), lambda bi, e, idx: (bi, e, 0, 0)),
            pl.BlockSpec((1, EXPERT_FF, D_MODEL), lambda bi, e, idx: (e, 0, 0)),
            pl.BlockSpec((1, 1, cap, 1), lambda bi, e, idx: (bi, e, 0, 0)),
            pl.BlockSpec((1, 1, D_MODEL), lambda bi, e, idx: (bi, 0, 0)),
            pl.BlockSpec(memory_space=pl.ANY),
        ],
        out_specs=pl.BlockSpec((1, n, D_MODEL), lambda bi, e, idx: (bi, 0, 0)),
        scratch_shapes=[pltpu.VMEM((cap, D_MODEL), F32), pltpu.VMEM((cap, D_MODEL), F32),
                        pltpu.SemaphoreType.DMA(())],
    )
    return pl.pallas_call(
        functools.partial(_down_scatter_kernel, cap),
        grid_spec=grid_spec,
        out_shape=jax.ShapeDtypeStruct((b, n, D_MODEL), F32),
        compiler_params=_params(("arbitrary", "arbitrary")),
        name="expert_down_scatter",
    )(idx_flat, hid, wd, gates.reshape(b, N_EXPERTS, cap, 1), gt2, x1)


def _combine_kernel(ek, x_ref, gt_ref, ys_ref, idx_ref, gate_ref, o_ref, acc_ref):
    e = pl.program_id(2)

    @pl.when(e == 0)
    def _():
        acc_ref[...] = jnp.zeros(acc_ref.shape, F32)

    n = acc_ref.shape[0]
    cap = idx_ref.shape[-1]
    tok = lax.broadcasted_iota(jnp.int32, (n, cap), 0)
    for j in range(ek):
        onehot = jnp.where(tok == idx_ref[0, j], 1.0, 0.0).astype(BF16)
        yg = (ys_ref[0, j] * gate_ref[0, j]).astype(BF16)
        acc_ref[...] += jnp.dot(onehot, yg, preferred_element_type=F32)

    @pl.when(e == pl.num_programs(2) - 1)
    def _():
        o_ref[0] = x_ref[0] + gt_ref[0] * acc_ref[...]


def moe_combine(x1, gt2, ys, idx, gates, td, ek):
    b, n, _ = x1.shape
    cap = idx.shape[-1]
    return pl.pallas_call(
        functools.partial(_combine_kernel, ek),
        grid=(b, D_MODEL // td, N_EXPERTS // ek),
        in_specs=[
            pl.BlockSpec((1, n, td), lambda bi, d, e: (bi, 0, d)),
            pl.BlockSpec((1, 1, td), lambda bi, d, e: (bi, 0, d)),
            pl.BlockSpec((1, ek, cap, td), lambda bi, d, e: (bi, e, 0, d)),
            pl.BlockSpec((1, ek, 1, cap), lambda bi, d, e: (bi, e, 0, 0)),
            pl.BlockSpec((1, ek, cap, 1), lambda bi, d, e: (bi, e, 0, 0)),
        ],
        out_specs=pl.BlockSpec((1, n, td), lambda bi, d, e: (bi, 0, d)),
        out_shape=jax.ShapeDtypeStruct((b, n, D_MODEL), F32),
        scratch_shapes=[pltpu.VMEM((n, td), F32)],
        compiler_params=_params(("arbitrary", "arbitrary", "arbitrary")),
        name="moe_combine",
    )(x1, gt2, ys, idx.reshape(b, N_EXPERTS, 1, cap), gates.reshape(b, N_EXPERTS, cap, 1))


def _final_norm_kernel(x_ref, g_ref, o_ref):
    x = x_ref[0]
    ms = jnp.mean(x * x, axis=-1, keepdims=True)
    o_ref[0] = x * lax.rsqrt(ms + EPS) * g_ref[...]


def final_norm(x, g, tm=512):
    b, n, _ = x.shape
    tm = min(tm, n)
    row = pl.BlockSpec((1, tm, D_MODEL), lambda bi, i: (bi, i, 0))
    return pl.pallas_call(
        _final_norm_kernel,
        grid=(b, n // tm),
        in_specs=[row, pl.BlockSpec((1, D_MODEL), lambda bi, i: (0, 0))],
        out_specs=row,
        out_shape=jax.ShapeDtypeStruct(x.shape, F32),
        compiler_params=_params(("arbitrary", "arbitrary")),
        name="final_norm",
    )(x, g.reshape(1, D_MODEL))


def rope_tables(n):
    pos = jnp.arange(n, dtype=jnp.int32)
    rc = jnp.stack([pos // GRID_W, pos % GRID_W], axis=-1).astype(F32)
    inv_freq = ROPE_THETA ** (-jnp.arange(ROPE_FREQS, dtype=F32) / ROPE_FREQS)
    ang = rc[:, :, None, None] * inv_freq
    ang = jnp.broadcast_to(ang, (n, 2, 2, ROPE_FREQS)).reshape(n, HEAD_D)
    ang = jnp.concatenate([ang, ang], axis=-1)
    first_half = (jnp.arange(HEAD_W) % (2 * ROPE_FREQS)) < ROPE_FREQS
    sin = jnp.sin(ang)
    return jnp.cos(ang), jnp.where(first_half, -sin, 0.0), jnp.where(first_half, 0.0, sin)


def moe_latent(x1, gt2, hp, aff, wg, wu, wd):
    b, n, _ = hp.shape
    cap = (CAPACITY_FACTOR * n) // N_EXPERTS
    idx, gates = route(aff, cap)
    hid = expert_ffn(hp, jnp.swapaxes(idx, 0, 1).reshape(-1), wg, wu, None, cap, group=1)
    return expert_down_scatter(hid, idx.reshape(-1), gates, wd, x1, gt2)


def moe_context(x1, gt2, hp, aff, wg, wu, wd, group):
    b, n, _ = hp.shape
    cap = (CAPACITY_FACTOR * n) // N_EXPERTS
    idx, gates = route(aff, cap)
    ys = expert_ffn(hp, jnp.swapaxes(idx, 0, 1).reshape(-1), wg, wu, wd, cap, group)
    return moe_combine(x1, gt2, ys, idx, gates, td=1024, ek=N_EXPERTS)


def kernel(x, c, ctx, c_ctx, w_ada, b_ada, g_norm1, w_in, diff_lambda, g_sub, conv_w, conv_b, conv_ln_g,
           conv_ln_b, w_out, g_norm2, w_router, w_gate, w_up, w_down, g_final):
    b, n, d = x.shape
    nc = ctx.shape[1]
    pad_rows = (-(b + 1)) % 8
    cc = jnp.concatenate([c, c_ctx[None, :], jnp.zeros((pad_rows, d), F32)], axis=0)
    mod_all = ada_modulation(cc, w_ada, b_ada)

    cos, sa, sb = rope_tables(n)
    cos_c = jnp.ones((nc, HEAD_W), F32)
    zero_c = jnp.zeros((nc, HEAD_W), F32)
    cs_n, wc = dft_tables(n)
    cs_c, _ = dft_tables(nc)

    xc = ctx
    for l in range(DEPTH):
        last = l == DEPTH - 1
        lam_init = 0.8 - 0.6 * math.exp(-0.3 * l)
        mod = mod_all[l, :b].reshape(b, 1, 6 * d)
        modc = jnp.broadcast_to(mod_all[l, b].reshape(1, 1, 6 * d), (b, 1, 6 * d))
        sh1, sc1, gt1, sh2, sc2, gt2 = [mod[..., i * d:(i + 1) * d] for i in range(6)]
        csh1, csc1, cgt1, csh2, csc2, cgt2 = [modc[..., i * d:(i + 1) * d] for i in range(6)]
        g1 = g_norm1[l].reshape(1, d)
        g2 = g_norm2[l].reshape(1, d)
        w_in_bf = w_in[l].astype(BF16)
        w_out_bf = w_out[l].astype(BF16)
        wr_pad = jnp.pad(w_router[l], ((0, 0), (0, ROUTER_LANES - N_EXPERTS))).astype(BF16)
        wg, wu, wd = w_gate[l].astype(BF16), w_up[l].astype(BF16), w_down[l].astype(BF16)
        wvt_bf = w_in[l][:, V0:G0].T.astype(BF16)
        gs = g_sub[l].reshape(HEAD_W, 1)
        conv_args = (conv_w[l], conv_b[l], conv_ln_g[l], conv_ln_b[l])

        qc, kc, vct, zc, ufc = inproj(xc, csh1, csc1, g1, w_in_bf, wvt_bf, cos_c, zero_c, zero_c)
        q, k_all, vt_all, z, uf = inproj(x, sh1, sc1, g1, w_in_bf, wvt_bf, cos, sa, sb, kv_prefix=(kc, vct))

        attn = diff_attention(q, k_all, vt_all, diff_lambda[l], gs, lam_init, tq=2 * ATTN_CHUNK)
        conv = conformer_conv(z, *conv_args)
        four = fourier_mix(uf, wc, cs_n)
        x1, hp, aff = outproj_router(attn, conv, four, w_out_bf, x, gt1, g2, sh2, sc2, wr_pad)
        x = moe_latent(x1, gt2, hp, aff, wg, wu, wd)

        if not last:
            attn_c = diff_attention(qc, kc, vct, diff_lambda[l], gs, lam_init, tq=ATTN_CHUNK)
            conv_c = conformer_conv(zc, *conv_args)
            four_c = fourier_mix(ufc, wc, cs_c)
            xc1, hpc, affc = outproj_router(attn_c, conv_c, four_c, w_out_bf, xc, cgt1, g2, csh2, csc2, wr_pad)
            xc = moe_context(xc1, cgt2, hpc, affc, wg, wu, wd, group=8)
    return final_norm(x, g_final)
```

```python
import functools
import math

import jax
import jax.numpy as jnp
from jax import lax
from jax.experimental import pallas as pl
from jax.experimental.pallas import tpu as pltpu

D_MODEL = 2048
DEPTH = 4
GRID_W = 64
N_HEADS = 8
HEAD_D = 64
HEAD_W = 2 * HEAD_D
ATTN_W = N_HEADS * HEAD_W
QK_W = ATTN_W
ROPE_THETA = 10000.0
ROPE_FREQS = HEAD_D // 4
CONV_W = 512
CONV_K = 31
CONV_PAD = 16
FOUR_HEADS = 4
FOUR_HD = 128
FOUR_W = FOUR_HEADS * FOUR_HD
K0 = QK_W
V0 = 2 * QK_W
G0 = V0 + ATTN_W
F0 = G0 + 2 * CONV_W
IN_W = F0 + FOUR_W
N_EXPERTS = 16
EXPERT_FF = 1024
CAPACITY_FACTOR = 2
EPS = 1e-6
ROUTER_LANES = 128
HALF_D = D_MODEL // 2
GATHER_UNROLL = 8
ATTN_SCORE_LEAD = 2
ROUTE_GROUP = 1
ONES_ROWS = 16
ATTN_CHUNK = 256

VMEM_LIMIT = 56 * 1024 * 1024

F32 = jnp.float32
BF16 = jnp.bfloat16
NT_DIMS = (((1,), (1,)), ((), ()))


def _params(sem, vmem=VMEM_LIMIT):
    return pltpu.CompilerParams(dimension_semantics=sem, vmem_limit_bytes=vmem)


def _resident(shape, index_map):
    return pl.BlockSpec(shape, index_map, pipeline_mode=pl.Buffered(1))


def _ada_kernel(c_ref, w_ref, b_ref, o_ref):
    c = c_ref[...]
    a = (c * jax.nn.sigmoid(c)).astype(BF16)
    w = w_ref[0].astype(BF16)
    o_ref[0] = jnp.dot(a, w, preferred_element_type=F32) + b_ref[0]


def ada_modulation(cc, w_ada, b_ada):
    rows = cc.shape[0]
    tn = 1024
    return pl.pallas_call(
        _ada_kernel,
        grid=(DEPTH, 6 * D_MODEL // tn),
        in_specs=[
            pl.BlockSpec((rows, D_MODEL), lambda l, j: (0, 0)),
            pl.BlockSpec((1, D_MODEL, tn), lambda l, j: (l, 0, j)),
            pl.BlockSpec((1, 1, tn), lambda l, j: (l, 0, j)),
        ],
        out_specs=pl.BlockSpec((1, rows, tn), lambda l, j: (l, 0, j)),
        out_shape=jax.ShapeDtypeStruct((DEPTH, rows, 6 * D_MODEL), F32),
        compiler_params=_params(("arbitrary", "arbitrary")),
        name="ada_modulation",
    )(cc, w_ada, b_ada.reshape(DEPTH, 1, 6 * D_MODEL))


def _rms_mod(x, g, sc, sh):
    ms = jnp.mean(x * x, axis=-1, keepdims=True)
    return (x * lax.rsqrt(ms + EPS)) * g * (1.0 + sc) + sh


def _inproj_kernel(nprefix, *refs):
    if nprefix == 0:
        _inproj_body(*refs)
        return
    ins, (kc_ref, vtc_ref), outs = refs[:9], refs[9:11], refs[11:]
    i = pl.program_id(1)

    @pl.when(i < nprefix)
    def _():
        outs[1][...] = kc_ref[...]
        outs[2][...] = vtc_ref[...]

    @pl.when(i >= nprefix)
    def _():
        _inproj_body(*ins, *outs)


def _inproj_body(x_ref, sh_ref, sc_ref, g_ref, w_ref, wvt_ref, cos_ref, sa_ref, sb_ref,
                 q_ref, k_ref, vt_ref, z_ref, f_ref):
    hb = _rms_mod(x_ref[0], g_ref[...], sc_ref[0], sh_ref[0]).astype(BF16)
    cos = cos_ref[...]
    sa = sa_ref[...]
    sb = sb_ref[...]

    def rope_store(col0, out_ref, scale):
        t = jnp.dot(hb, w_ref[:, col0:col0 + QK_W], preferred_element_type=F32)
        for h in range(N_HEADS):
            th = t[:, h * HEAD_W:(h + 1) * HEAD_W]
            r = th * cos + pltpu.roll(th, HEAD_W - ROPE_FREQS, 1) * sa + pltpu.roll(th, ROPE_FREQS, 1) * sb
            out_ref[0, :, h * HEAD_W:(h + 1) * HEAD_W] = (r * scale).astype(BF16)

    rope_store(0, q_ref, HEAD_D ** -0.5 * math.log2(math.e))
    rope_store(K0, k_ref, 1.0)
    vt = lax.dot_general(wvt_ref[...], hb, NT_DIMS, preferred_element_type=F32).astype(BF16)
    for h in range(N_HEADS):
        vt_ref[0, h, 0, 0:HEAD_W, :] = vt[h * HEAD_W:(h + 1) * HEAD_W, :]
        vt_ref[0, h, 0, HEAD_W:HEAD_W + ONES_ROWS, :] = jnp.ones((ONES_ROWS, ATTN_CHUNK), BF16)
    a = jnp.dot(hb, w_ref[:, G0:G0 + CONV_W], preferred_element_type=F32)
    g = jnp.dot(hb, w_ref[:, G0 + CONV_W:F0], preferred_element_type=F32)
    z_ref[0] = a * jax.nn.sigmoid(g)
    f_ref[0] = jnp.dot(hb, w_ref[:, F0:IN_W], preferred_element_type=F32).astype(BF16)


def inproj(x, sh, sc, g, w_in_bf, wvt_bf, cos, sa, sb, kv_prefix=None):
    b, n, _ = x.shape
    tm = ATTN_CHUNK
    nprefix = 0 if kv_prefix is None else kv_prefix[0].shape[1] // ATTN_CHUNK
    nchunk = nprefix + n // ATTN_CHUNK
    step = lambda i: jnp.maximum(i - nprefix, 0)
    row = lambda bi, i: (bi, step(i), 0)
    vec = lambda bi, i: (bi, 0, 0)
    tab = lambda bi, i: (step(i), 0)
    out = lambda w, dt: jax.ShapeDtypeStruct((b, n, w), dt)
    in_specs = [
        pl.BlockSpec((1, tm, D_MODEL), row),
        pl.BlockSpec((1, 1, D_MODEL), vec),
        pl.BlockSpec((1, 1, D_MODEL), vec),
        pl.BlockSpec((1, D_MODEL), lambda bi, i: (0, 0)),
        _resident((D_MODEL, IN_W), lambda bi, i: (0, 0)),
        _resident((ATTN_W, D_MODEL), lambda bi, i: (0, 0)),
        pl.BlockSpec((tm, HEAD_W), tab),
        pl.BlockSpec((tm, HEAD_W), tab),
        pl.BlockSpec((tm, HEAD_W), tab),
    ]
    args = [x, sh, sc, g, w_in_bf, wvt_bf, cos, sa, sb]
    if nprefix:
        pre = lambda i: jnp.minimum(i, nprefix - 1)
        in_specs += [pl.BlockSpec((1, tm, QK_W), lambda bi, i: (bi, pre(i), 0)),
                     pl.BlockSpec((1, N_HEADS, 1, HEAD_W + ONES_ROWS, tm), lambda bi, i: (bi, 0, pre(i), 0, 0))]
        args += list(kv_prefix)
    return pl.pallas_call(
        functools.partial(_inproj_kernel, nprefix),
        grid=(b, nchunk),
        in_specs=in_specs,
        out_specs=[
            pl.BlockSpec((1, tm, QK_W), row),
            pl.BlockSpec((1, tm, QK_W), lambda bi, i: (bi, i, 0)),
            pl.BlockSpec((1, N_HEADS, 1, HEAD_W + ONES_ROWS, tm), lambda bi, i: (bi, 0, i, 0, 0)),
            pl.BlockSpec((1, tm, CONV_W), row),
            pl.BlockSpec((1, tm, FOUR_W), row),
        ],
        out_shape=[out(QK_W, BF16),
                   jax.ShapeDtypeStruct((b, nchunk * ATTN_CHUNK, QK_W), BF16),
                   jax.ShapeDtypeStruct((b, N_HEADS, nchunk, HEAD_W + ONES_ROWS, ATTN_CHUNK), BF16),
                   out(CONV_W, F32), out(FOUR_W, BF16)],
        compiler_params=_params(("arbitrary", "arbitrary")),
        name="inproj",
    )(*args)


def _attn_kernel(nchunk, lam_init, q_ref, lam_ref, gsub_ref, k_ref, vt_ref, o_ref):
    tq = q_ref.shape[1]
    ck = ATTN_CHUNK
    lv = lam_ref[...]
    lam = (jnp.exp(jnp.sum(lv[0:1] * lv[1:2], axis=-1, keepdims=True))
           - jnp.exp(jnp.sum(lv[2:3] * lv[3:4], axis=-1, keepdims=True)) + lam_init)
    chains = [(c, m) for c in range(tq // ck) for m in range(2)]

    def scores(c, m):
        q = q_ref[0, c * ck:(c + 1) * ck, :].astype(F32)
        lane = lax.broadcasted_iota(jnp.int32, q.shape, 1)
        qm = jnp.where((lane < HEAD_D) if m == 0 else (lane >= HEAD_D), q, 0.0).astype(BF16)
        return lax.dot_general(k_ref[0], qm, NT_DIMS, preferred_element_type=F32)

    def probs(s):
        mx = jnp.max(s, axis=0, keepdims=True)
        return jnp.exp2(s - mx).astype(BF16)

    def values(p):
        return functools.reduce(jnp.add, [jnp.dot(vt_ref[0, 0, j], p[j * ck:(j + 1) * ck], preferred_element_type=F32)
                                          for j in range(nchunk)])

    s_of, p_of, o_of = {}, {}, {}
    lead = ATTN_SCORE_LEAD
    for t in range(len(chains) + lead + 1):
        if t < len(chains):
            s_of[t] = scores(*chains[t])
        if lead <= t < len(chains) + lead:
            p_of[t - lead] = probs(s_of.pop(t - lead))
        if t >= lead + 1:
            o_of[t - lead - 1] = values(p_of.pop(t - lead - 1))
    for c in range(tq // ck):
        o0, o1 = o_of[2 * c], o_of[2 * c + 1]
        ot = (o0[:HEAD_W] * (1.0 / o0[HEAD_W:HEAD_W + 1]) - o1[:HEAD_W] * (lam / o1[HEAD_W:HEAD_W + 1]))
        ms = jnp.mean(ot * ot, axis=0, keepdims=True)
        ot = (ot * lax.rsqrt(ms + EPS)) * gsub_ref[...] * (1.0 - lam_init)
        o_ref[0, c * ck:(c + 1) * ck, :] = ot.T.astype(BF16)


def diff_attention(q, k_all, vt_all, lam_vec, g_sub_col, lam_init, tq):
    b, n, _ = q.shape
    nkeys = k_all.shape[1]
    nchunk = nkeys // ATTN_CHUNK
    return pl.pallas_call(
        functools.partial(_attn_kernel, nchunk, lam_init),
        grid=(b, N_HEADS, n // tq),
        in_specs=[
            pl.BlockSpec((1, tq, HEAD_W), lambda bi, h, i: (bi, i, h)),
            pl.BlockSpec((4, HEAD_D), lambda bi, h, i: (0, 0)),
            pl.BlockSpec((HEAD_W, 1), lambda bi, h, i: (0, 0)),
            pl.BlockSpec((1, nkeys, HEAD_W), lambda bi, h, i: (bi, 0, h)),
            pl.BlockSpec((1, 1, nchunk, HEAD_W + ONES_ROWS, ATTN_CHUNK), lambda bi, h, i: (bi, h, 0, 0, 0)),
        ],
        out_specs=pl.BlockSpec((1, tq, HEAD_W), lambda bi, h, i: (bi, i, h)),
        out_shape=jax.ShapeDtypeStruct((b, n, ATTN_W), BF16),
        compiler_params=_params(("arbitrary", "arbitrary", "arbitrary")),
        name="diff_attention",
    )(q, lam_vec, g_sub_col, k_all, vt_all)


def _conv_kernel(n, rows, z_ref, w_ref, b_ref, lg_ref, lb_ref, o_ref, zp_ref, y_ref):
    zp_ref[0:CONV_PAD, :] = jnp.zeros((CONV_PAD, CONV_W), F32)
    zp_ref[CONV_PAD + n:2 * CONV_PAD + n, :] = jnp.zeros((CONV_PAD, CONV_W), F32)
    zp_ref[CONV_PAD:CONV_PAD + n, :] = z_ref[0]
    first = CONV_PAD - CONV_K // 2

    def chunk(i, carry):
        r0 = pl.multiple_of(i * rows, rows)
        for c in range(CONV_W // 128):
            lanes = slice(c * 128, (c + 1) * 128)
            total = rows + 2 * CONV_PAD
            win = zp_ref[pl.ds(r0, total), lanes]
            shifted = [win] + [pltpu.roll(win, total - s, 0) for s in range(1, 8)]
            acc = jnp.zeros((rows, 128), F32)
            for k in range(CONV_K):
                tiles, s = divmod(first + k, 8)
                acc = acc + shifted[s][8 * tiles:8 * tiles + rows, :] * w_ref[k:k + 1, lanes]
            y_ref[pl.ds(r0, rows), lanes] = acc + b_ref[:, lanes]
        return carry

    lax.fori_loop(0, n // rows, chunk, 0)
    y = y_ref[...]
    mu = jnp.mean(y, axis=-1, keepdims=True)
    d = y - mu
    var = jnp.mean(d * d, axis=-1, keepdims=True)
    yn = d * lax.rsqrt(var + EPS) * lg_ref[...] + lb_ref[...]
    o_ref[0] = (yn * jax.nn.sigmoid(yn)).astype(BF16)


def conformer_conv(z, w, bias, ln_g, ln_b):
    b, n, _ = z.shape
    vec = pl.BlockSpec((1, CONV_W), lambda bi: (0, 0))
    return pl.pallas_call(
        functools.partial(_conv_kernel, n, 64),
        grid=(b,),
        in_specs=[
            pl.BlockSpec((1, n, CONV_W), lambda bi: (bi, 0, 0)),
            pl.BlockSpec((CONV_K, CONV_W), lambda bi: (0, 0)),
            vec, vec, vec,
        ],
        out_specs=pl.BlockSpec((1, n, CONV_W), lambda bi: (bi, 0, 0)),
        out_shape=jax.ShapeDtypeStruct((b, n, CONV_W), BF16),
        scratch_shapes=[pltpu.VMEM((n + 2 * CONV_PAD, CONV_W), F32), pltpu.VMEM((n, CONV_W), F32)],
        compiler_params=_params(("arbitrary",)),
        name="conformer_conv",
    )(z, w, bias.reshape(1, CONV_W), ln_g.reshape(1, CONV_W), ln_b.reshape(1, CONV_W))


def _fourier_kernel(n, u_ref, wc_ref, cs_ref, o_ref, pq_ref):
    @pl.when(pl.program_id(1) == 0)
    def _():
        pq = jnp.dot(u_ref[0], wc_ref[...], preferred_element_type=F32)
        pq_ref[0:n, :] = pq[:, :FOUR_W].astype(BF16)
        pq_ref[n:2 * n, :] = pq[:, FOUR_W:].astype(BF16)

    o_ref[0] = jnp.dot(cs_ref[...], pq_ref[...], preferred_element_type=F32).astype(BF16)


def fourier_mix(u, wc, cs, tm=256):
    b, n, _ = u.shape
    tm = min(tm, n)
    return pl.pallas_call(
        functools.partial(_fourier_kernel, n),
        grid=(b, n // tm),
        in_specs=[
            pl.BlockSpec((1, n, FOUR_W), lambda bi, i: (bi, 0, 0)),
            pl.BlockSpec((FOUR_W, 2 * FOUR_W), lambda bi, i: (0, 0)),
            pl.BlockSpec((tm, 2 * n), lambda bi, i: (i, 0)),
        ],
        out_specs=pl.BlockSpec((1, tm, FOUR_W), lambda bi, i: (bi, i, 0)),
        out_shape=jax.ShapeDtypeStruct((b, n, FOUR_W), BF16),
        scratch_shapes=[pltpu.VMEM((2 * n, FOUR_W), BF16)],
        compiler_params=_params(("arbitrary", "arbitrary")),
        name="fourier_mix",
    )(u, wc, cs)


def dft_tables(n):
    j = jnp.arange(n, dtype=jnp.int32)
    ang = (2.0 * math.pi / n) * ((j[:, None] * j[None, :]) % n).astype(F32)
    cs = jnp.concatenate([jnp.cos(ang), -jnp.sin(ang)], axis=1) * (n ** -0.5)
    c = jnp.arange(FOUR_HD, dtype=jnp.int32)
    angc = (2.0 * math.pi / FOUR_HD) * ((c[:, None] * c[None, :]) % FOUR_HD).astype(F32)
    eye = jnp.eye(FOUR_HEADS, dtype=F32)
    wc = jnp.concatenate([jnp.kron(eye, jnp.cos(angc)), jnp.kron(eye, jnp.sin(angc))], axis=1) * (FOUR_HD ** -0.5)
    return cs.astype(BF16), wc.astype(BF16)


def _outproj_kernel(a_ref, c_ref, f_ref, w_ref, x_ref, gt_ref, g2_ref, sh_ref, sc_ref, wr_ref,
                    x1_ref, hp_ref, aff_ref):
    y = jnp.dot(a_ref[0], w_ref[0:ATTN_W, :], preferred_element_type=F32)
    y = y + jnp.dot(c_ref[0], w_ref[ATTN_W:ATTN_W + CONV_W, :], preferred_element_type=F32)
    y = y + jnp.dot(f_ref[0], w_ref[ATTN_W + CONV_W:, :], preferred_element_type=F32)
    x1 = x_ref[0] + gt_ref[0] * y
    x1_ref[0] = x1
    hb = _rms_mod(x1, g2_ref[...], sc_ref[0], sh_ref[0]).astype(BF16)
    bits = pltpu.bitcast(hb.astype(F32), jnp.uint32)
    hp_ref[0] = (bits[:, :HALF_D] >> 16) | bits[:, HALF_D:]
    logits = jnp.dot(hb, wr_ref[...], preferred_element_type=F32)
    lane = lax.broadcasted_iota(jnp.int32, logits.shape, 1)
    logits = jnp.where(lane < N_EXPERTS, logits, -1e30)
    ex = jnp.exp(logits - jnp.max(logits, axis=-1, keepdims=True))
    aff_ref[0] = ex / jnp.sum(ex, axis=-1, keepdims=True)


def outproj_router(attn, conv, four, w_out_bf, x, gt1, g2, sh2, sc2, wr_pad, tm=256):
    b, n, _ = x.shape
    row = lambda bi, i: (bi, i, 0)
    vec = lambda bi, i: (bi, 0, 0)
    return pl.pallas_call(
        _outproj_kernel,
        grid=(b, n // tm),
        in_specs=[
            pl.BlockSpec((1, tm, ATTN_W), row),
            pl.BlockSpec((1, tm, CONV_W), row),
            pl.BlockSpec((1, tm, FOUR_W), row),
            _resident((D_MODEL, D_MODEL), lambda bi, i: (0, 0)),
            pl.BlockSpec((1, tm, D_MODEL), row),
            pl.BlockSpec((1, 1, D_MODEL), vec),
            pl.BlockSpec((1, D_MODEL), lambda bi, i: (0, 0)),
            pl.BlockSpec((1, 1, D_MODEL), vec),
            pl.BlockSpec((1, 1, D_MODEL), vec),
            pl.BlockSpec((D_MODEL, ROUTER_LANES), lambda bi, i: (0, 0)),
        ],
        out_specs=[
            pl.BlockSpec((1, tm, D_MODEL), row),
            pl.BlockSpec((1, tm, HALF_D), row),
            pl.BlockSpec((1, tm, ROUTER_LANES), row),
        ],
        out_shape=[
            jax.ShapeDtypeStruct((b, n, D_MODEL), F32),
            jax.ShapeDtypeStruct((b, n, HALF_D), jnp.uint32),
            jax.ShapeDtypeStruct((b, n, ROUTER_LANES), F32),
        ],
        compiler_params=_params(("arbitrary", "arbitrary")),
        name="outproj_router",
    )(attn, conv, four, w_out_bf, x, gt1, g2, sh2, sc2, wr_pad)


def _prefix_count(x01, tri):
    carry = jnp.zeros((x01.shape[0], 1), F32)
    parts = []
    for c in range(x01.shape[1] // 128):
        loc = jnp.dot(x01[:, c * 128:(c + 1) * 128].astype(BF16), tri, preferred_element_type=F32)
        parts.append(loc + carry)
        carry = carry + loc[:, 127:128]
    return jnp.concatenate(parts, axis=1)


def _route_kernel(cap, aff_ref, idx_ref, gate_ref):
    group, n, _ = aff_ref.shape
    rows = group * N_EXPERTS
    a = jnp.concatenate([aff_ref[g].T[0:N_EXPERTS, :] for g in range(group)], axis=0)
    bits = pltpu.bitcast(a, jnp.int32)
    thr = jnp.zeros((rows, 1), jnp.int32)
    for bit in range(30, -1, -1):
        cand = thr | (1 << bit)
        cnt = jnp.sum(jnp.where(bits >= cand, 1.0, 0.0), axis=1, keepdims=True)
        thr = jnp.where(cnt >= cap, cand, thr)
    above = jnp.where(bits > thr, 1.0, 0.0)
    tied = jnp.where(bits == thr, 1.0, 0.0)
    r = lax.broadcasted_iota(jnp.int32, (128, 128), 0)
    c = lax.broadcasted_iota(jnp.int32, (128, 128), 1)
    tri = jnp.where(r <= c, 1.0, 0.0).astype(BF16)
    need = cap - jnp.sum(above, axis=1, keepdims=True)
    sel = above + tied * jnp.where(_prefix_count(tied, tri) <= need, 1.0, 0.0)
    slot_of = jnp.where(sel > 0.0, _prefix_count(sel, tri) - 1.0, -1.0)
    pad = jnp.full((ROUTER_LANES - rows, n), -1.0, F32)
    slot_t = jnp.concatenate([slot_of, pad], axis=0).T if rows < ROUTER_LANES else slot_of.T
    slot = lax.broadcasted_iota(jnp.int32, (n, cap), 1).astype(F32)
    tok = lax.broadcasted_iota(jnp.int32, (n, cap), 0)
    for g in range(group):
        aff = aff_ref[g]
        for e in range(N_EXPERTS):
            col = g * N_EXPERTS + e
            hit = slot_t[:, col:col + 1] == slot
            idx_ref[g, e:e + 1, :] = jnp.sum(jnp.where(hit, tok, 0), axis=0, keepdims=True)
            gate_ref[g, e:e + 1, :] = jnp.sum(jnp.where(hit, aff[:, e:e + 1], 0.0), axis=0, keepdims=True)


def route(aff, cap):
    b, n, _ = aff.shape
    return pl.pallas_call(
        functools.partial(_route_kernel, cap),
        grid=(b // ROUTE_GROUP,),
        in_specs=[pl.BlockSpec((ROUTE_GROUP, n, ROUTER_LANES), lambda bi: (bi, 0, 0))],
        out_specs=[pl.BlockSpec((ROUTE_GROUP, N_EXPERTS, cap), lambda bi: (bi, 0, 0))] * 2,
        out_shape=[jax.ShapeDtypeStruct((b, N_EXPERTS, cap), jnp.int32),
                   jax.ShapeDtypeStruct((b, N_EXPERTS, cap), F32)],
        compiler_params=_params(("arbitrary",)),
        name="route",
    )(aff)


def _ffn_kernel(group, cap, with_down, idx_ref, src_ref, wg_ref, wu_ref, *refs):
    wd_ref = refs[0] if with_down else None
    o_ref, xs_ref = refs[-2:]
    e = pl.program_id(0)
    bi = pl.program_id(1)
    base = (e * pl.num_programs(1) + bi) * (group * cap)
    for g in range(group):
        def gather_row(p, carry, g=g):
            tok = idx_ref[base + g * cap + p]
            xs_ref[pl.ds(g * cap + p, 1), :] = src_ref[g, pl.ds(tok, 1), :]
            return carry
        lax.fori_loop(0, cap, gather_row, 0, unroll=GATHER_UNROLL)
    w = xs_ref[...]
    lo = pltpu.bitcast(w << 16, F32).astype(BF16)
    hi = pltpu.bitcast(w & jnp.uint32(0xFFFF0000), F32).astype(BF16)

    def proj(w_ref):
        return (jnp.dot(lo, w_ref[0, :HALF_D, :], preferred_element_type=F32)
                + jnp.dot(hi, w_ref[0, HALF_D:, :], preferred_element_type=F32))

    gate = proj(wg_ref)
    hid = (gate * jax.nn.sigmoid(gate) * proj(wu_ref)).astype(BF16)
    res = jnp.dot(hid, wd_ref[0], preferred_element_type=F32) if with_down else hid
    for g in range(group):
        o_ref[g, 0] = res[g * cap:(g + 1) * cap, :]


def expert_ffn(hp, idx_flat, wg, wu, wd, cap, group):
    b, n, _ = hp.shape
    nb = b // group
    with_down = wd is not None
    out_w, out_dt = (D_MODEL, F32) if with_down else (EXPERT_FF, BF16)
    in_specs = [
        pl.BlockSpec((group, n, HALF_D), lambda e, bi, idx: (bi, 0, 0)),
        pl.BlockSpec((1, D_MODEL, EXPERT_FF), lambda e, bi, idx: (e, 0, 0)),
        pl.BlockSpec((1, D_MODEL, EXPERT_FF), lambda e, bi, idx: (e, 0, 0)),
    ]
    args = [idx_flat, hp, wg, wu]
    if with_down:
        in_specs.append(pl.BlockSpec((1, EXPERT_FF, D_MODEL), lambda e, bi, idx: (e, 0, 0)))
        args.append(wd)
    grid_spec = pltpu.PrefetchScalarGridSpec(
        num_scalar_prefetch=1,
        grid=(N_EXPERTS, nb),
        in_specs=in_specs,
        out_specs=pl.BlockSpec((group, 1, cap, out_w), lambda e, bi, idx: (bi, e, 0, 0)),
        scratch_shapes=[pltpu.VMEM((group * cap, HALF_D), jnp.uint32)],
    )
    return pl.pallas_call(
        functools.partial(_ffn_kernel, group, cap, with_down),
        grid_spec=grid_spec,
        out_shape=jax.ShapeDtypeStruct((b, N_EXPERTS, cap, out_w), out_dt),
        compiler_params=_params(("arbitrary", "arbitrary")),
        name="expert_ffn",
    )(*args)


def _down_scatter_kernel(cap, idx_ref, hid_ref, wd_ref, gate_ref, gt_ref, x1_hbm, o_ref, ya_ref, yb_ref, sem):
    bi = pl.program_id(0)
    e = pl.program_id(1)

    @pl.when(e == 0)
    def _():
        load = pltpu.make_async_copy(x1_hbm.at[bi], o_ref.at[0], sem)
        load.start()
        load.wait()
        yb_ref[...] = jnp.zeros(yb_ref.shape, F32)

    base = (bi * N_EXPERTS + e) * cap
    prev = jnp.maximum(base - cap, 0)

    def add_row(y_ref, at, p):
        tok = idx_ref[at + p]
        o_ref[0, pl.ds(tok, 1), :] = o_ref[0, pl.ds(tok, 1), :] + y_ref[pl.ds(p, 1), :]

    def step(y_new, y_old):
        for p in range(cap):
            add_row(y_old, prev, p)
        y = jnp.dot(hid_ref[0, 0], wd_ref[0], preferred_element_type=F32)
        y_new[...] = y * gate_ref[0, 0] * gt_ref[0]

        @pl.when(e == pl.num_programs(1) - 1)
        def _():
            def body(p, carry):
                add_row(y_new, base, p)
                return carry
            lax.fori_loop(0, cap, body, 0, unroll=GATHER_UNROLL)

    @pl.when(e % 2 == 0)
    def _():
        step(ya_ref, yb_ref)

    @pl.when(e % 2 == 1)
    def _():
        step(yb_ref, ya_ref)


def expert_down_scatter(hid, idx_flat, gates, wd, x1, gt2):
    b, n, _ = x1.shape
    cap = hid.shape[2]
    grid_spec = pltpu.PrefetchScalarGridSpec(
        num_scalar_prefetch=1,
        grid=(b, N_EXPERTS),
        in_specs=[
            pl.BlockSpec((1, 1, cap, EXPERT_FF), lambda bi, e, idx: (bi, e, 0, 0)),
            pl.BlockSpec((1, EXPERT_FF, D_MODEL), lambda bi, e, idx: (e, 0, 0)),
            pl.BlockSpec((1, 1, cap, 1), lambda bi, e, idx: (bi, e, 0, 0)),
            pl.BlockSpec((1, 1, D_MODEL), lambda bi, e, idx: (bi, 0, 0)),
            pl.BlockSpec(memory_space=pl.ANY),
        ],
        out_specs=pl.BlockSpec((1, n, D_MODEL), lambda bi, e, idx: (bi, 0, 0)),
        scratch_shapes=[pltpu.VMEM((cap, D_MODEL), F32), pltpu.VMEM((cap, D_MODEL), F32),
                        pltpu.SemaphoreType.DMA(())],
    )
    return pl.pallas_call(
        functools.partial(_down_scatter_kernel, cap),
        grid_spec=grid_spec,
        out_shape=jax.ShapeDtypeStruct((b, n, D_MODEL), F32),
        compiler_params=_params(("arbitrary", "arbitrary")),
        name="expert_down_scatter",
    )(idx_flat, hid, wd, gates.reshape(b, N_EXPERTS, cap, 1), gt2, x1)


def _combine_kernel(ek, x_ref, gt_ref, ys_ref, idx_ref, gate_ref, o_ref, acc_ref):
    e = pl.program_id(2)

    @pl.when(e == 0)
    def _():
        acc_ref[...] = jnp.zeros(acc_ref.shape, F32)

    n = acc_ref.shape[0]
    cap = idx_ref.shape[-1]
    tok = lax.broadcasted_iota(jnp.int32, (n, cap), 0)
    for j in range(ek):
        onehot = jnp.where(tok == idx_ref[0, j], 1.0, 0.0).astype(BF16)
        yg = (ys_ref[0, j] * gate_ref[0, j]).astype(BF16)
        acc_ref[...] += jnp.dot(onehot, yg, preferred_element_type=F32)

    @pl.when(e == pl.num_programs(2) - 1)
    def _():
        o_ref[0] = x_ref[0] + gt_ref[0] * acc_ref[...]


def moe_combine(x1, gt2, ys, idx, gates, td, ek):
    b, n, _ = x1.shape
    cap = idx.shape[-1]
    return pl.pallas_call(
        functools.partial(_combine_kernel, ek),
        grid=(b, D_MODEL // td, N_EXPERTS // ek),
        in_specs=[
            pl.BlockSpec((1, n, td), lambda bi, d, e: (bi, 0, d)),
            pl.BlockSpec((1, 1, td), lambda bi, d, e: (bi, 0, d)),
            pl.BlockSpec((1, ek, cap, td), lambda bi, d, e: (bi, e, 0, d)),
            pl.BlockSpec((1, ek, 1, cap), lambda bi, d, e: (bi, e, 0, 0)),
            pl.BlockSpec((1, ek, cap, 1), lambda bi, d, e: (bi, e, 0, 0)),
        ],
        out_specs=pl.BlockSpec((1, n, td), lambda bi, d, e: (bi, 0, d)),
        out_shape=jax.ShapeDtypeStruct((b, n, D_MODEL), F32),
        scratch_shapes=[pltpu.VMEM((n, td), F32)],
        compiler_params=_params(("arbitrary", "arbitrary", "arbitrary")),
        name="moe_combine",
    )(x1, gt2, ys, idx.reshape(b, N_EXPERTS, 1, cap), gates.reshape(b, N_EXPERTS, cap, 1))


def _final_norm_kernel(x_ref, g_ref, o_ref):
    x = x_ref[0]
    ms = jnp.mean(x * x, axis=-1, keepdims=True)
    o_ref[0] = x * lax.rsqrt(ms + EPS) * g_ref[...]


def final_norm(x, g, tm=512):
    b, n, _ = x.shape
    tm = min(tm, n)
    row = pl.BlockSpec((1, tm, D_MODEL), lambda bi, i: (bi, i, 0))
    return pl.pallas_call(
        _final_norm_kernel,
        grid=(b, n // tm),
        in_specs=[row, pl.BlockSpec((1, D_MODEL), lambda bi, i: (0, 0))],
        out_specs=row,
        out_shape=jax.ShapeDtypeStruct(x.shape, F32),
        compiler_params=_params(("arbitrary", "arbitrary")),
        name="final_norm",
    )(x, g.reshape(1, D_MODEL))


def rope_tables(n):
    pos = jnp.arange(n, dtype=jnp.int32)
    rc = jnp.stack([pos // GRID_W, pos % GRID_W], axis=-1).astype(F32)
    inv_freq = ROPE_THETA ** (-jnp.arange(ROPE_FREQS, dtype=F32) / ROPE_FREQS)
    ang = rc[:, :, None, None] * inv_freq
    ang = jnp.broadcast_to(ang, (n, 2, 2, ROPE_FREQS)).reshape(n, HEAD_D)
    ang = jnp.concatenate([ang, ang], axis=-1)
    first_half = (jnp.arange(HEAD_W) % (2 * ROPE_FREQS)) < ROPE_FREQS
    sin = jnp.sin(ang)
    return jnp.cos(ang), jnp.where(first_half, -sin, 0.0), jnp.where(first_half, 0.0, sin)


def moe_latent(x1, gt2, hp, aff, wg, wu, wd):
    b, n, _ = hp.shape
    cap = (CAPACITY_FACTOR * n) // N_EXPERTS
    idx, gates = route(aff, cap)
    hid = expert_ffn(hp, jnp.swapaxes(idx, 0, 1).reshape(-1), wg, wu, None, cap, group=1)
    return expert_down_scatter(hid, idx.reshape(-1), gates, wd, x1, gt2)


def moe_context(x1, gt2, hp, aff, wg, wu, wd, group):
    b, n, _ = hp.shape
    cap = (CAPACITY_FACTOR * n) // N_EXPERTS
    idx, gates = route(aff, cap)
    ys = expert_ffn(hp, jnp.swapaxes(idx, 0, 1).reshape(-1), wg, wu, wd, cap, group)
    return moe_combine(x1, gt2, ys, idx, gates, td=1024, ek=N_EXPERTS)


def kernel(x, c, ctx, c_ctx, w_ada, b_ada, g_norm1, w_in, diff_lambda, g_sub, conv_w, conv_b, conv_ln_g,
           conv_ln_b, w_out, g_norm2, w_router, w_gate, w_up, w_down, g_final):
    b, n, d = x.shape
    nc = ctx.shape[1]
    pad_rows = (-(b + 1)) % 8
    cc = jnp.concatenate([c, c_ctx[None, :], jnp.zeros((pad_rows, d), F32)], axis=0)
    mod_all = ada_modulation(cc, w_ada, b_ada)

    cos, sa, sb = rope_tables(n)
    cos_c = jnp.ones((nc, HEAD_W), F32)
    zero_c = jnp.zeros((nc, HEAD_W), F32)
    cs_n, wc = dft_tables(n)
    cs_c, _ = dft_tables(nc)

    xc = ctx
    for l in range(DEPTH):
        last = l == DEPTH - 1
        lam_init = 0.8 - 0.6 * math.exp(-0.3 * l)
        mod = mod_all[l, :b].reshape(b, 1, 6 * d)
        modc = jnp.broadcast_to(mod_all[l, b].reshape(1, 1, 6 * d), (b, 1, 6 * d))
        sh1, sc1, gt1, sh2, sc2, gt2 = [mod[..., i * d:(i + 1) * d] for i in range(6)]
        csh1, csc1, cgt1, csh2, csc2, cgt2 = [modc[..., i * d:(i + 1) * d] for i in range(6)]
        g1 = g_norm1[l].reshape(1, d)
        g2 = g_norm2[l].reshape(1, d)
        w_in_bf = w_in[l].astype(BF16)
        w_out_bf = w_out[l].astype(BF16)
        wr_pad = jnp.pad(w_router[l], ((0, 0), (0, ROUTER_LANES - N_EXPERTS))).astype(BF16)
        wg, wu, wd = w_gate[l].astype(BF16), w_up[l].astype(BF16), w_down[l].astype(BF16)
        wvt_bf = w_in[l][:, V0:G0].T.astype(BF16)
        gs = g_sub[l].reshape(HEAD_W, 1)
        conv_args = (conv_w[l], conv_b[l], conv_ln_g[l], conv_ln_b[l])

        qc, kc, vct, zc, ufc = inproj(xc, csh1, csc1, g1, w_in_bf, wvt_bf, cos_c, zero_c, zero_c)
        q, k_all, vt_all, z, uf = inproj(x, sh1, sc1, g1, w_in_bf, wvt_bf, cos, sa, sb, kv_prefix=(kc, vct))

        attn = diff_attention(q, k_all, vt_all, diff_lambda[l], gs, lam_init, tq=2 * ATTN_CHUNK)
        conv = conformer_conv(z, *conv_args)
        four = fourier_mix(uf, wc, cs_n)
        x1, hp, aff = outproj_router(attn, conv, four, w_out_bf, x, gt1, g2, sh2, sc2, wr_pad)
        x = moe_latent(x1, gt2, hp, aff, wg, wu, wd)

        if not last:
            attn_c = diff_attention(qc, kc, vct, diff_lambda[l], gs, lam_init, tq=ATTN_CHUNK)
            conv_c = conformer_conv(zc, *conv_args)
            four_c = fourier_mix(ufc, wc, cs_c)
            xc1, hpc, affc = outproj_router(attn_c, conv_c, four_c, w_out_bf, xc, cgt1, g2, csh2, csc2, wr_pad)
            xc = moe_context(xc1, cgt2, hpc, affc, wg, wu, wd, group=8)
    return final_norm(x, g_final)
```
